```python
import math
import jax, jax.numpy as jnp
from jax import lax
import numpy as np

D_MODEL = 1024
BATCH = 8
SEQ = 4096
DEPTH = 1

MIX_WIDTH = D_MODEL
DIFF_HEADS = 4
DIFF_V_DIM = MIX_WIDTH // 2 // DIFF_HEADS
DIFF_QK_DIM = DIFF_V_DIM // 2
RET_HEADS = 4
RET_V_DIM = MIX_WIDTH // 2 // RET_HEADS
RET_QK_DIM = RET_V_DIM // 2
Q_BLOCK = 128
RET_CHUNK = 128
ROPE_BASE = 10000.0
T5_BUCKETS = 32
T5_MAX_EXACT = 16
T5_MAX_DISTANCE = 128
PEER_HEADS = 8
PEER_N_KEYS = 128
PEER_N_EXPERTS = PEER_N_KEYS * PEER_N_KEYS
PEER_TOPK = 16
PEER_QUERY_DIM = 256
PEER_KEY_DIM = PEER_QUERY_DIM // 2
PEER_CHUNK = 128
PLE_DIM = 256

kernel_name = "hymba_diffattn_retnet_peer_block"

_SPLIT_SIZES = [
    2 * DIFF_HEADS * DIFF_QK_DIM,
    2 * DIFF_HEADS * DIFF_QK_DIM,
    DIFF_HEADS * DIFF_V_DIM,
    RET_HEADS * RET_QK_DIM,
    RET_HEADS * RET_QK_DIM,
    RET_HEADS * RET_V_DIM,
    RET_HEADS * RET_V_DIM,
]
IN_WIDTH = sum(_SPLIT_SIZES)


def rmsnorm(x, gain=None, eps=1e-6):
    xf = x.astype(jnp.float32)
    y = xf * lax.rsqrt(jnp.mean(xf * xf, axis=-1, keepdims=True) + eps)
    if gain is not None:
        y = y * gain.astype(jnp.float32)
    return y.astype(x.dtype)


def t5_bucket(n):
    exact = n < T5_MAX_EXACT
    nf = jnp.maximum(n, 1).astype(jnp.float32)
    large = T5_MAX_EXACT + (jnp.log(nf / T5_MAX_EXACT)
                            / math.log(T5_MAX_DISTANCE / T5_MAX_EXACT)
                            * (T5_BUCKETS - T5_MAX_EXACT)).astype(jnp.int32)
    large = jnp.minimum(large, T5_BUCKETS - 1)
    return jnp.where(exact, n, large)


def rotary(x, pos):
    half = x.shape[-1] // 2
    freqs = 1.0 / (ROPE_BASE ** (jnp.arange(half, dtype=jnp.float32) / half))
    ang = pos[:, None] * freqs[None, :]
    cos, sin = jnp.cos(ang), jnp.sin(ang)
    xf = x.astype(jnp.float32)
    x1, x2 = xf[..., :half], xf[..., half:]
    return jnp.concatenate([x1 * cos - x2 * sin, x1 * sin + x2 * cos], axis=-1).astype(x.dtype)


def diff_attention(q, k, v, lam, rel_bias):
    S = q.shape[3]
    scale = DIFF_QK_DIM ** -0.5
    outs = []
    for blk in range(S // Q_BLOCK):
        start = blk * Q_BLOCK
        end = start + Q_BLOCK
        qb = q[:, :, :, start:end]
        kb = k[:, :, :, :end]
        vb = v[:, :, :end]
        dist = (start + jnp.arange(Q_BLOCK, dtype=jnp.int32))[:, None] - jnp.arange(end, dtype=jnp.int32)[None, :]
        bias = jnp.transpose(rel_bias[t5_bucket(jnp.maximum(dist, 0))], (2, 0, 1)).astype(jnp.float32)
        logits = jnp.einsum('bmhqd,bmhkd->bmhqk', qb, kb).astype(jnp.float32) * scale + bias
        logits = jnp.where(dist >= 0, logits, -jnp.inf)
        probs = jax.nn.softmax(logits, axis=-1)
        attn = probs[:, 0] - lam * probs[:, 1]
        outs.append(jnp.einsum('bhqk,bhkd->bhqd', attn.astype(v.dtype), vb))
    return jnp.concatenate(outs, axis=2)


def retention(q, k, v, log_gamma):
    B, H, S, dk = q.shape
    dv = v.shape[-1]
    C = RET_CHUNK
    N = S // C
    qc = q.reshape(B, H, N, C, dk)
    kc = k.reshape(B, H, N, C, dk)
    vc = v.reshape(B, H, N, C, dv)
    i = jnp.arange(C, dtype=jnp.float32)
    rel = i[:, None] - i[None, :]
    decay = jnp.where(rel >= 0, jnp.exp(jnp.maximum(rel, 0.0)[None] * log_gamma[:, None, None]), 0.0)
    zeta = jnp.exp((C - 1 - i)[None, :] * log_gamma[:, None])
    xi = jnp.exp((i + 1)[None, :] * log_gamma[:, None])
    chunk_decay = jnp.exp(C * log_gamma)
    scores = jnp.einsum('bhnid,bhnjd->bhnij', qc, kc) * decay[None, :, None]
    inner = jnp.einsum('bhnij,bhnje->bhnie', scores, vc)
    kv = jnp.einsum('bhnjd,bhnje->nbhde', kc * zeta[None, :, None, :, None], vc)

    def step(state, kv_n):
        return chunk_decay[None, :, None, None] * state + kv_n, state

    _, state_prev = lax.scan(step, jnp.zeros(kv.shape[1:], kv.dtype), kv)
    cross = jnp.einsum('bhnid,nbhde->bhnie', qc * xi[None, :, None, :, None], state_prev)
    return (inner + cross).reshape(B, H, S, dv).astype(v.dtype)


def peer(m, w_query, sub_keys, u, v):
    B, S, D = m.shape
    qry = (m @ w_query).reshape(B, S, PEER_HEADS, 2, PEER_KEY_DIM)
    scores = jnp.einsum('bshcd,hckd->bshck', qry, sub_keys).astype(jnp.float32)
    vals, idx = lax.top_k(scores, PEER_TOPK)
    cand = vals[..., 0, :, None] + vals[..., 1, None, :]
    cand_idx = idx[..., 0, :, None] * PEER_N_KEYS + idx[..., 1, None, :]
    cand = cand.reshape(B, S, PEER_HEADS, PEER_TOPK * PEER_TOPK)
    cand_idx = cand_idx.reshape(B, S, PEER_HEADS, PEER_TOPK * PEER_TOPK)
    top_s, top_pos = lax.top_k(cand, PEER_TOPK)
    expert = jnp.take_along_axis(cand_idx, top_pos, axis=-1)
    gates = jax.nn.softmax(top_s, axis=-1)
    T = B * S
    n_sel = PEER_HEADS * PEER_TOPK
    tokens = m.reshape(T // PEER_CHUNK, PEER_CHUNK, D)
    expert = expert.reshape(T // PEER_CHUNK, PEER_CHUNK, n_sel)
    gates = gates.reshape(T // PEER_CHUNK, PEER_CHUNK, n_sel)

    def chunk(args):
        xt, e, g = args
        u_sel = jnp.take(u, e, axis=0)
        act = jax.nn.gelu(jnp.einsum('cd,ced->ce', xt, u_sel).astype(jnp.float32), approximate=False) * g
        v_sel = jnp.take(v, e, axis=0)
        return jnp.einsum('ce,ced->cd', act.astype(xt.dtype), v_sel)

    out = lax.map(chunk, (tokens, expert, gates))
    return out.reshape(B, S, D)


def setup_inputs(seed: int = 0) -> dict:
    key = jax.random.key(seed)
    ks = jax.random.split(key, 24)
    f32 = jnp.float32
    nrm = lambda k, shape, s: (jax.random.normal(k, shape, f32) * s)
    return {
        "x": nrm(ks[0], (BATCH, SEQ, D_MODEL), 1.0),
        "p": nrm(ks[1], (DEPTH, BATCH, SEQ, PLE_DIM), 1.0),
        "attn_norm": 1.0 + nrm(ks[2], (DEPTH, D_MODEL), 0.02),
        "w_in": nrm(ks[3], (DEPTH, D_MODEL, IN_WIDTH), D_MODEL ** -0.5),
        "lam_q1": nrm(ks[4], (DEPTH, DIFF_QK_DIM), 0.1),
        "lam_k1": nrm(ks[5], (DEPTH, DIFF_QK_DIM), 0.1),
        "lam_q2": nrm(ks[6], (DEPTH, DIFF_QK_DIM), 0.1),
        "lam_k2": nrm(ks[7], (DEPTH, DIFF_QK_DIM), 0.1),
        "subln_gain": 1.0 + nrm(ks[8], (DEPTH, DIFF_V_DIM), 0.02),
        "w_out": nrm(ks[9], (DEPTH, MIX_WIDTH, D_MODEL), MIX_WIDTH ** -0.5),
        "rel_bias": nrm(ks[10], (T5_BUCKETS, DIFF_HEADS), 0.2),
        "ffn_norm": 1.0 + nrm(ks[11], (DEPTH, D_MODEL), 0.02),
        "peer_query": nrm(ks[12], (DEPTH, D_MODEL, PEER_HEADS * PEER_QUERY_DIM), D_MODEL ** -0.5),
        "peer_subkeys": nrm(ks[13], (DEPTH, PEER_HEADS, 2, PEER_N_KEYS, PEER_KEY_DIM), PEER_KEY_DIM ** -0.5),
        "peer_u": nrm(ks[14], (DEPTH, PEER_N_EXPERTS, D_MODEL), D_MODEL ** -0.5),
        "peer_v": nrm(ks[15], (DEPTH, PEER_N_EXPERTS, D_MODEL), (PEER_HEADS * PEER_TOPK) ** -0.5),
        "ple_norm": 1.0 + nrm(ks[16], (DEPTH, D_MODEL), 0.02),
        "ple_gate_w": nrm(ks[17], (DEPTH, D_MODEL, D_MODEL), D_MODEL ** -0.5),
        "ple_gate_b": nrm(ks[18], (DEPTH, D_MODEL), 0.01),
        "ple_proj": nrm(ks[19], (DEPTH, PLE_DIM, D_MODEL), PLE_DIM ** -0.5),
        "final_norm": 1.0 + nrm(ks[20], (D_MODEL,), 0.02),
    }


def reference(x, p, attn_norm, w_in, lam_q1, lam_k1, lam_q2, lam_k2, subln_gain, w_out,
              rel_bias, ffn_norm, peer_query, peer_subkeys, peer_u, peer_v,
              ple_norm, ple_gate_w, ple_gate_b, ple_proj, final_norm):
    B, S, D = x.shape
    pos = jnp.arange(S, dtype=jnp.float32)
    gamma = 1.0 - jnp.exp2(-5.0 - jnp.arange(RET_HEADS, dtype=jnp.float32))
    log_gamma = jnp.log(gamma)
    split_at = [int(s) for s in np.cumsum(_SPLIT_SIZES)[:-1]]
    h = x
    for layer in range(DEPTH):
        a = rmsnorm(h, attn_norm[layer])
        proj = a @ w_in[layer]
        dq, dk, dv, rq, rk, rv, rg = jnp.split(proj, split_at, axis=-1)

        dq = dq.reshape(B, S, 2, DIFF_HEADS, DIFF_QK_DIM).transpose(0, 2, 3, 1, 4)
        dk = dk.reshape(B, S, 2, DIFF_HEADS, DIFF_QK_DIM).transpose(0, 2, 3, 1, 4)
        dv = dv.reshape(B, S, DIFF_HEADS, DIFF_V_DIM).transpose(0, 2, 1, 3)
        lambda_init = 0.8 - 0.6 * math.exp(-0.3 * layer)
        lam = (jnp.exp(jnp.sum(lam_q1[layer].astype(jnp.float32) * lam_k1[layer].astype(jnp.float32)))
               - jnp.exp(jnp.sum(lam_q2[layer].astype(jnp.float32) * lam_k2[layer].astype(jnp.float32)))
               + lambda_init)
        o = diff_attention(dq, dk, dv, lam, rel_bias)
        o = rmsnorm(o, subln_gain[layer], eps=1e-5) * (1.0 - lambda_init)
        diff_out = o.transpose(0, 2, 1, 3).reshape(B, S, DIFF_HEADS * DIFF_V_DIM)

        rq = rotary(rq.reshape(B, S, RET_HEADS, RET_QK_DIM).transpose(0, 2, 1, 3), pos)
        rk = rotary(rk.reshape(B, S, RET_HEADS, RET_QK_DIM).transpose(0, 2, 1, 3), pos) * (RET_QK_DIM ** -0.5)
        rv = rv.reshape(B, S, RET_HEADS, RET_V_DIM).transpose(0, 2, 1, 3)
        r = rmsnorm(retention(rq, rk, rv, log_gamma))
        ret_out = r.transpose(0, 2, 1, 3).reshape(B, S, RET_HEADS * RET_V_DIM) * jax.nn.silu(rg)

        h = h + jnp.concatenate([diff_out, ret_out], axis=-1) @ w_out[layer]

        m = rmsnorm(h, ffn_norm[layer])
        h = h + peer(m, peer_query[layer], peer_subkeys[layer], peer_u[layer], peer_v[layer])

        n = rmsnorm(h, ple_norm[layer])
        gate = jax.nn.sigmoid(n @ ple_gate_w[layer] + ple_gate_b[layer])
        h = h + gate * (p[layer] @ ple_proj[layer])
    return rmsnorm(h, final_norm)
```

```python
import functools
import math

import numpy as np
import jax
import jax.numpy as jnp
from jax import lax
from jax.experimental import pallas as pl
from jax.experimental.pallas import tpu as pltpu

F32 = jnp.float32
BF16 = jnp.bfloat16
I32 = jnp.int32

DIFF_HEADS = 4
DIFF_V_DIM = 128
DIFF_QK_DIM = 64
RET_HEADS = 4
RET_V_DIM = 128
RET_QK_DIM = 64
RET_CHUNK = 128
ROPE_BASE = 10000.0
T5_BUCKETS = 32
T5_MAX_EXACT = 16
T5_MAX_DISTANCE = 128
PEER_HEADS = 8
PEER_N_KEYS = 128
PEER_TOPK = 16
PEER_KEY_DIM = 128
N_SEL = PEER_HEADS * PEER_TOPK

LANES = 128
SUBLANES = 8
VMEM_LIMIT = 48 * 1024 * 1024
VMEM_LIMIT_TABLE = 56 * 1024 * 1024

ROW_TILE = 256
ATT_TILE = 256
TOPK_TILE = 256
PEER_TILE = 64
WORDS_PER_ROW = 4
SLOT_STRIDE = N_SEL + SUBLANES
NEG_BIG = -1e30


def _cparams(sem, limit=VMEM_LIMIT):
    return pltpu.CompilerParams(dimension_semantics=sem, vmem_limit_bytes=limit)


def _dot(a, b):
    return jnp.dot(a, b, preferred_element_type=F32)


def _dot_nt(a, b):
    return lax.dot_general(a, b, (((1,), (1,)), ((), ())), preferred_element_type=F32)


def _rms(x, eps):
    return x * lax.rsqrt(jnp.mean(x * x, axis=-1, keepdims=True) + eps)


def _in_proj_kernel(x_ref, g_ref, wa_ref, wr_ref, oa_ref, or_ref):
    a = (_rms(x_ref[...], 1e-6) * g_ref[...]).astype(BF16)
    oa_ref[...] = _dot(a, wa_ref[...]).astype(BF16)
    or_ref[...] = _dot(a, wr_ref[...])


def _in_proj(x2, gain, wa, wr):
    t, d = x2.shape
    na, nr = wa.shape[1], wr.shape[1]
    return pl.pallas_call(
        _in_proj_kernel,
        grid=(t // ROW_TILE,),
        in_specs=[
            pl.BlockSpec((ROW_TILE, d), lambda i: (i, 0)),
            pl.BlockSpec((1, d), lambda i: (0, 0)),
            pl.BlockSpec((d, na), lambda i: (0, 0)),
            pl.BlockSpec((d, nr), lambda i: (0, 0)),
        ],
        out_specs=[
            pl.BlockSpec((ROW_TILE, na), lambda i: (i, 0)),
            pl.BlockSpec((ROW_TILE, nr), lambda i: (i, 0)),
        ],
        out_shape=[jax.ShapeDtypeStruct((t, na), BF16), jax.ShapeDtypeStruct((t, nr), F32)],
        compiler_params=_cparams(("parallel",)),
        name="in_proj",
    )(x2, gain, wa, wr)


def _diff_attn_kernel(q_ref, k_ref, v_ref, bias_ref, lam_ref, g_ref, o_ref, *, lambda_init):
    i = pl.program_id(2)
    tq = q_ref.shape[0]
    tk = tq
    q = q_ref[...] * jnp.asarray(DIFF_QK_DIM ** -0.5, BF16)
    lane = lax.broadcasted_iota(I32, q.shape, 1)
    zero = jnp.zeros_like(q)
    q1 = jnp.where(lane < DIFF_QK_DIM, q, zero)
    q2 = jnp.where(lane >= DIFF_QK_DIM, q, zero)

    def online(s, m, l, acc, v):
        mn = jnp.maximum(m, jnp.max(s, axis=-1, keepdims=True))
        p = jnp.exp(s - mn)
        alpha = jnp.exp(m - mn)
        l = alpha * l + jnp.sum(p, axis=-1, keepdims=True)
        acc = alpha * acc + _dot(p.astype(BF16), v)
        return mn, l, acc

    def body(j, carry):
        m1, l1, a1, m2, l2, a2 = carry
        off = pl.multiple_of(j * tk, tk)
        k = k_ref[pl.ds(off, tk), :]
        v = v_ref[pl.ds(off, tk), :]
        bias = bias_ref[jnp.minimum(i - j, 2)]
        s1 = _dot_nt(q1, k) + bias
        s2 = _dot_nt(q2, k) + bias
        m1, l1, a1 = online(s1, m1, l1, a1, v)
        m2, l2, a2 = online(s2, m2, l2, a2, v)
        return m1, l1, a1, m2, l2, a2

    mi = jnp.full((tq, 1), NEG_BIG, F32)
    li = jnp.zeros((tq, 1), F32)
    ai = jnp.zeros((tq, DIFF_V_DIM), F32)
    m1, l1, a1, m2, l2, a2 = lax.fori_loop(0, i + 1, body, (mi, li, ai, mi, li, ai))

    lv = lam_ref[...]
    lam = (jnp.exp(jnp.sum(lv[0:1] * lv[1:2], axis=-1, keepdims=True))
           - jnp.exp(jnp.sum(lv[2:3] * lv[3:4], axis=-1, keepdims=True)) + lambda_init)
    o = a1 / l1 - lam * (a2 / l2)
    o = _rms(o, 1e-5) * g_ref[...] * (1.0 - lambda_init)
    o_ref[...] = o.astype(o_ref.dtype)


def _t5_bucket(n):
    exact = n < T5_MAX_EXACT
    nf = jnp.maximum(n, 1).astype(F32)
    large = T5_MAX_EXACT + (jnp.log(nf / T5_MAX_EXACT)
                            / math.log(T5_MAX_DISTANCE / T5_MAX_EXACT)
                            * (T5_BUCKETS - T5_MAX_EXACT)).astype(I32)
    large = jnp.minimum(large, T5_BUCKETS - 1)
    return jnp.where(exact, n, large)


def _bias_tiles(rel_bias, t):
    assert t >= T5_MAX_DISTANCE
    r = jnp.arange(t, dtype=I32)[:, None]
    c = jnp.arange(t, dtype=I32)[None, :]
    d0 = r - c
    diag = jnp.where((d0 >= 0)[..., None], rel_bias[_t5_bucket(jnp.maximum(d0, 0))].astype(F32), NEG_BIG)
    near = rel_bias[_t5_bucket(d0 + t)].astype(F32)
    far = jnp.broadcast_to(rel_bias[_t5_bucket(jnp.full((1, 1), 2 * t, I32))].astype(F32), near.shape)
    return jnp.transpose(jnp.stack([diag, near, far], axis=0), (3, 0, 1, 2))


def _diff_attn(qkv, bias, lamv, gain, lambda_init):
    b, s, _ = qkv.shape
    h = DIFF_HEADS
    t = ATT_TILE
    kern = functools.partial(_diff_attn_kernel, lambda_init=lambda_init)
    return pl.pallas_call(
        kern,
        grid=(b, h, s // t),
        in_specs=[
            pl.BlockSpec((None, t, LANES), lambda bi, hi, i: (bi, i, hi)),
            pl.BlockSpec((None, s, LANES), lambda bi, hi, i: (bi, 0, h + hi)),
            pl.BlockSpec((None, s, LANES), lambda bi, hi, i: (bi, 0, 2 * h + hi)),
            pl.BlockSpec((None, 3, t, t), lambda bi, hi, i: (hi, 0, 0, 0)),
            pl.BlockSpec((4, DIFF_QK_DIM), lambda bi, hi, i: (0, 0)),
            pl.BlockSpec((1, DIFF_V_DIM), lambda bi, hi, i: (0, 0)),
        ],
        out_specs=pl.BlockSpec((None, t, LANES), lambda bi, hi, i: (bi, i, hi)),
        out_shape=jax.ShapeDtypeStruct((b, s, h * DIFF_V_DIM), BF16),
        compiler_params=_cparams(("parallel", "parallel", "arbitrary")),
        name="diff_attn",
    )(qkv, qkv, qkv, bias, lamv, gain)


def _retention_kernel(q_ref, k_ref, v_ref, g_ref, cos_ref, sin_ref, decay_ref, zeta_ref, xi_ref,
                      cd_ref, o_ref, state_ref):
    @pl.when(pl.program_id(1) == 0)
    def _():
        state_ref[...] = jnp.zeros_like(state_ref)

    cos = cos_ref[...]
    sin = sin_ref[...]

    def rotate(x):
        x1 = x[:, :LANES]
        x2 = x[:, LANES:]
        return jnp.concatenate([x1 * cos - x2 * sin, x1 * sin + x2 * cos], axis=1)

    qr = rotate(q_ref[...])
    kr = rotate(k_ref[...]) * (RET_QK_DIM ** -0.5)
    qx = qr * xi_ref[...]
    kz = kr * zeta_ref[...]
    lane = lax.broadcasted_iota(I32, qr.shape, 1)
    head_of_lane = (lane & (LANES - 1)) >> 5
    zero = jnp.zeros_like(qr)
    for h in range(RET_HEADS):
        msk = head_of_lane == h
        qh = jnp.where(msk, qr, zero).astype(BF16)
        kh = jnp.where(msk, kr, zero).astype(BF16)
        vh = v_ref[:, h * RET_V_DIM:(h + 1) * RET_V_DIM].astype(BF16)
        scores = _dot_nt(qh, kh) * decay_ref[h]
        inner = _dot(scores.astype(BF16), vh)
        st = state_ref[h]
        cross = _dot(jnp.where(msk, qx, zero).astype(BF16), st.astype(BF16))
        kzh = jnp.where(msk, kz, zero).astype(BF16)
        kv = lax.dot_general(kzh, vh, (((0,), (0,)), ((), ())), preferred_element_type=F32)
        state_ref[h] = cd_ref[h] * st + kv
        r = _rms(inner + cross, 1e-6)
        g = g_ref[:, h * RET_V_DIM:(h + 1) * RET_V_DIM]
        silu = g * (1.0 / (1.0 + jnp.exp(-g)))
        o_ref[:, h * RET_V_DIM:(h + 1) * RET_V_DIM] = (r * silu).astype(o_ref.dtype)


def _retention(pr, s):
    b = pr.shape[0]
    c = RET_CHUNK
    hh = RET_HEADS
    half = RET_QK_DIM // 2
    gamma = 1.0 - jnp.exp2(-5.0 - jnp.arange(hh, dtype=F32))
    lg = jnp.log(gamma)
    i = jnp.arange(c, dtype=F32)
    rel = i[:, None] - i[None, :]
    decay = jnp.where(rel >= 0, jnp.exp(jnp.maximum(rel, 0.0)[None] * lg[:, None, None]), 0.0)
    zeta = jnp.exp((c - 1 - i)[None, :] * lg[:, None])
    xi = jnp.exp((i + 1)[None, :] * lg[:, None])
    cd = jnp.exp(c * lg)
    head_of_lane = (np.arange(2 * LANES) % LANES) // half
    zeta_tab = zeta.T[:, head_of_lane]
    xi_tab = xi.T[:, head_of_lane]
    pos = jnp.arange(s, dtype=F32)
    freqs = 1.0 / (ROPE_BASE ** (jnp.arange(half, dtype=F32) / half))
    ang = pos[:, None] * freqs[None, :]
    cos_tab = jnp.tile(jnp.cos(ang), (1, hh))
    sin_tab = jnp.tile(jnp.sin(ang), (1, hh))
    w = hh * RET_V_DIM
    return pl.pallas_call(
        _retention_kernel,
        grid=(b, s // c),
        in_specs=[
            pl.BlockSpec((None, c, 2 * LANES), lambda bi, n: (bi, n, 0)),
            pl.BlockSpec((None, c, 2 * LANES), lambda bi, n: (bi, n, 1)),
            pl.BlockSpec((None, c, w), lambda bi, n: (bi, n, 1)),
            pl.BlockSpec((None, c, w), lambda bi, n: (bi, n, 2)),
            pl.BlockSpec((c, LANES), lambda bi, n: (n, 0)),
            pl.BlockSpec((c, LANES), lambda bi, n: (n, 0)),
            pl.BlockSpec((hh, c, c), lambda bi, n: (0, 0, 0)),
            pl.BlockSpec((c, 2 * LANES), lambda bi, n: (0, 0)),
            pl.BlockSpec((c, 2 * LANES), lambda bi, n: (0, 0)),
            pl.BlockSpec(memory_space=pltpu.SMEM),
        ],
        out_specs=pl.BlockSpec((None, c, w), lambda bi, n: (bi, n, 0)),
        out_shape=jax.ShapeDtypeStruct((b, s, w), BF16),
        scratch_shapes=[pltpu.VMEM((hh, 2 * LANES, RET_V_DIM), F32)],
        compiler_params=_cparams(("parallel", "arbitrary")),
        name="retention",
    )(pr, pr, pr, pr, cos_tab, sin_tab, decay, zeta_tab, xi_tab, cd)


def _out_proj_kernel(x_ref, d_ref, r_ref, wa_ref, wb_ref, g_ref, wq_ref, h_ref, m_ref, q_ref):
    h = x_ref[...] + _dot(d_ref[...], wa_ref[...]) + _dot(r_ref[...], wb_ref[...])
    h_ref[...] = h
    m = _rms(h, 1e-6) * g_ref[...]
    m_ref[...] = m
    q_ref[...] = _dot(m.astype(BF16), wq_ref[...]).astype(BF16)


def _out_proj(x2, d2, r2, wa, wb, gain, wq):
    t, d = x2.shape
    nq = wq.shape[1]
    kd, kr = d2.shape[1], r2.shape[1]
    row = lambda i: (i, 0)
    full = lambda i: (0, 0)
    return pl.pallas_call(
        _out_proj_kernel,
        grid=(t // ROW_TILE,),
        in_specs=[
            pl.BlockSpec((ROW_TILE, d), row),
            pl.BlockSpec((ROW_TILE, kd), row),
            pl.BlockSpec((ROW_TILE, kr), row),
            pl.BlockSpec((kd, d), full),
            pl.BlockSpec((kr, d), full),
            pl.BlockSpec((1, d), full),
            pl.BlockSpec((d, nq), full),
        ],
        out_specs=[
            pl.BlockSpec((ROW_TILE, d), row),
            pl.BlockSpec((ROW_TILE, d), row),
            pl.BlockSpec((ROW_TILE, nq), row),
        ],
        out_shape=[jax.ShapeDtypeStruct((t, d), F32), jax.ShapeDtypeStruct((t, d), F32),
                   jax.ShapeDtypeStruct((t, nq), BF16)],
        compiler_params=_cparams(("parallel",)),
        name="out_proj",
    )(x2, d2, r2, wa, wb, gain, wq)


def _topk_rows(s, k, fill):
    rows = lax.broadcasted_iota(I32, s.shape, 0)
    vals, ids = [], []
    for _ in range(k):
        mx = jnp.max(s, axis=0, keepdims=True)
        am = jnp.min(jnp.where(s == mx, rows, s.shape[0]), axis=0, keepdims=True)
        vals.append(mx)
        ids.append(am)
        s = jnp.where(rows == am, fill, s)
    return vals, ids


def _peer_topk_kernel(q_ref, sk_ref, idx_ref, gate_ref):
    k = PEER_TOPK
    neg_inf = jnp.asarray(-jnp.inf, F32)
    half_vals, half_ids = [], []
    for c in range(2):
        qc = q_ref[:, c * PEER_KEY_DIM:(c + 1) * PEER_KEY_DIM]
        s = _dot_nt(sk_ref[c], qc)
        vals, ids = _topk_rows(s, k, neg_inf)
        half_vals.append(jnp.concatenate(vals, axis=0))
        half_ids.append(jnp.concatenate(ids, axis=0))
    v1, v2 = half_vals
    i1, i2 = half_ids
    cand = jnp.concatenate([v1[a:a + 1] + v2 for a in range(k)], axis=0)
    cidx = jnp.concatenate([i1[a:a + 1] * PEER_N_KEYS + i2 for a in range(k)], axis=0)
    rows = lax.broadcasted_iota(I32, cand.shape, 0)
    top_s, top_e = [], []
    for _ in range(k):
        mx = jnp.max(cand, axis=0, keepdims=True)
        pos = jnp.min(jnp.where(cand == mx, rows, k * k), axis=0, keepdims=True)
        hit = rows == pos
        top_s.append(mx)
        top_e.append(jnp.max(jnp.where(hit, cidx, -1), axis=0, keepdims=True))
        cand = jnp.where(hit, neg_inf, cand)
    ts = jnp.concatenate(top_s, axis=0)
    ex = jnp.exp(ts - ts[0:1])
    gate_ref[...] = ex / jnp.sum(ex, axis=0, keepdims=True)
    idx_ref[...] = jnp.concatenate(top_e, axis=0)


def _peer_topk(qry, subkeys):
    t = qry.shape[0]
    tb = TOPK_TILE
    k = PEER_TOPK
    return pl.pallas_call(
        _peer_topk_kernel,
        grid=(t // tb, PEER_HEADS),
        in_specs=[
            pl.BlockSpec((tb, 2 * PEER_KEY_DIM), lambda i, h: (i, h)),
            pl.BlockSpec((2, PEER_N_KEYS, PEER_KEY_DIM), lambda i, h: (h, 0, 0)),
        ],
        out_specs=[
            pl.BlockSpec((k, tb), lambda i, h: (h, i)),
            pl.BlockSpec((k, tb), lambda i, h: (h, i)),
        ],
        out_shape=[jax.ShapeDtypeStruct((N_SEL, t), I32), jax.ShapeDtypeStruct((N_SEL, t), F32)],
        compiler_params=_cparams(("parallel", "parallel")),
        name="peer_topk",
    )(qry, subkeys)


def _pack_table(w):
    n, d = w.shape
    wb = lax.bitcast_convert_type(w.astype(BF16), jnp.uint16).astype(jnp.uint32)
    words = (wb[:, d // 2:] << 16) | wb[:, :d // 2]
    return lax.bitcast_convert_type(words, I32).reshape(n * WORDS_PER_ROW, LANES)


def _unpack(words):
    lo = lax.bitcast_convert_type(words << 16, F32)
    hi = lax.bitcast_convert_type(words & jnp.int32(-65536), F32)
    return lo, hi


def _peer_u_kernel(idx_ref, m_ref, gate_ref, tab_ref, w_ref, tile_ref, act_ref):
    tb = m_ref.shape[0]
    ones = jnp.ones((SUBLANES, LANES), BF16)

    def body(t, carry):
        for e in range(N_SEL):
            row = pl.multiple_of(idx_ref[t, e] * WORDS_PER_ROW, WORDS_PER_ROW)
            tile_ref[pl.ds(e, WORDS_PER_ROW, stride=SLOT_STRIDE), :] = tab_ref[pl.ds(row, WORDS_PER_ROW), :]
        xm = m_ref[t]
        acc = jnp.zeros((N_SEL, LANES), F32)
        for j in range(WORDS_PER_ROW):
            lo, hi = _unpack(tile_ref[pl.ds(j * SLOT_STRIDE, N_SEL), :])
            acc = acc + lo * xm[j:j + 1, :] + hi * xm[WORDS_PER_ROW + j:WORDS_PER_ROW + j + 1, :]
        a_hi = acc.astype(BF16)
        a_lo = (acc - a_hi.astype(F32)).astype(BF16)
        red = _dot_nt(ones, a_hi) + _dot_nt(ones, a_lo)
        act_ref[pl.ds(t, 1), :] = red[0:1]
        return carry

    lax.fori_loop(0, tb, body, 0)
    a = act_ref[...]
    gelu = 0.5 * a * (1.0 + lax.erf(a * (2.0 ** -0.5)))
    w_ref[...] = gelu * gate_ref[...]


def _peer_u(idx, m3, gates, table):
    t = idx.shape[0]
    tb = PEER_TILE
    return pl.pallas_call(
        _peer_u_kernel,
        grid=(t // tb,),
        in_specs=[
            pl.BlockSpec((tb, N_SEL), lambda i: (i, 0), memory_space=pltpu.SMEM),
            pl.BlockSpec((tb, SUBLANES, LANES), lambda i: (i, 0, 0)),
            pl.BlockSpec((tb, N_SEL), lambda i: (i, 0)),
            pl.BlockSpec(table.shape, lambda i: (0, 0)),
        ],
        out_specs=pl.BlockSpec((tb, N_SEL), lambda i: (i, 0)),
        out_shape=jax.ShapeDtypeStruct((t, N_SEL), F32),
        scratch_shapes=[pltpu.VMEM((WORDS_PER_ROW * SLOT_STRIDE, LANES), I32),
                        pltpu.VMEM((tb, N_SEL), F32)],
        compiler_params=_cparams(("arbitrary",), VMEM_LIMIT_TABLE),
        name="peer_u",
    )(idx, m3, gates, table)


def _peer_v_kernel(idx_ref, w_ref, h_ref, tab_ref, o_ref):
    tb = h_ref.shape[0]
    n_acc = 4

    def body(t, carry):
        lo_acc = [jnp.zeros((WORDS_PER_ROW, LANES), F32) for _ in range(n_acc)]
        hi_acc = [jnp.zeros((WORDS_PER_ROW, LANES), F32) for _ in range(n_acc)]
        for e in range(N_SEL):
            row = pl.multiple_of(idx_ref[t, e] * WORDS_PER_ROW, WORDS_PER_ROW)
            lo, hi = _unpack(tab_ref[pl.ds(row, WORDS_PER_ROW), :])
            wv = w_ref[t, e]
            lo_acc[e % n_acc] = lo_acc[e % n_acc] + lo * wv
            hi_acc[e % n_acc] = hi_acc[e % n_acc] + hi * wv
        lo = (lo_acc[0] + lo_acc[1]) + (lo_acc[2] + lo_acc[3])
        hi = (hi_acc[0] + hi_acc[1]) + (hi_acc[2] + hi_acc[3])
        o_ref[t] = h_ref[t] + jnp.concatenate([lo, hi], axis=0)
        return carry

    lax.fori_loop(0, tb, body, 0)


def _peer_v(idx, w, h3, table):
    t = idx.shape[0]
    tb = PEER_TILE
    return pl.pallas_call(
        _peer_v_kernel,
        grid=(t // tb,),
        in_specs=[
            pl.BlockSpec((tb, N_SEL), lambda i: (i, 0), memory_space=pltpu.SMEM),
            pl.BlockSpec((tb, N_SEL), lambda i: (i, 0), memory_space=pltpu.SMEM),
            pl.BlockSpec((tb, SUBLANES, LANES), lambda i: (i, 0, 0)),
            pl.BlockSpec(table.shape, lambda i: (0, 0)),
        ],
        out_specs=pl.BlockSpec((tb, SUBLANES, LANES), lambda i: (i, 0, 0)),
        out_shape=jax.ShapeDtypeStruct(h3.shape, F32),
        compiler_params=_cparams(("arbitrary",), VMEM_LIMIT_TABLE),
        name="peer_v",
    )(idx, w, h3, table)


def _ple_kernel(h_ref, p_ref, gn_ref, wg_ref, bg_ref, wp_ref, fn_ref, o_ref, *, final):
    h = h_ref[...]
    n = (_rms(h, 1e-6) * gn_ref[...]).astype(BF16)
    z = _dot(n, wg_ref[...]) + bg_ref[...]
    gate = 1.0 / (1.0 + jnp.exp(-z))
    h = h + gate * _dot(p_ref[...].astype(BF16), wp_ref[...])
    if final:
        h = _rms(h, 1e-6) * fn_ref[...]
    o_ref[...] = h


def _ple(h2, p2, gn, wg, bg, wp, fn, final):
    t, d = h2.shape
    pd = p2.shape[1]
    row = lambda i: (i, 0)
    full = lambda i: (0, 0)
    return pl.pallas_call(
        functools.partial(_ple_kernel, final=final),
        grid=(t // ROW_TILE,),
        in_specs=[
            pl.BlockSpec((ROW_TILE, d), row),
            pl.BlockSpec((ROW_TILE, pd), row),
            pl.BlockSpec((1, d), full),
            pl.BlockSpec((d, d), full),
            pl.BlockSpec((1, d), full),
            pl.BlockSpec((pd, d), full),
            pl.BlockSpec((1, d), full),
        ],
        out_specs=pl.BlockSpec((ROW_TILE, d), row),
        out_shape=jax.ShapeDtypeStruct((t, d), F32),
        compiler_params=_cparams(("parallel",)),
        name="ple_final",
    )(h2, p2, gn, wg, bg, wp, fn)


def _in_proj_perms():
    hq, dq = DIFF_HEADS, DIFF_QK_DIM
    qk = np.zeros(2 * hq * dq, np.int64)
    for h in range(hq):
        for m in range(2):
            qk[h * 2 * dq + m * dq:h * 2 * dq + (m + 1) * dq] = m * hq * dq + h * dq + np.arange(dq)
    n_qk = 2 * hq * dq
    n_v = hq * DIFF_V_DIM
    perm_a = np.concatenate([qk, n_qk + qk, 2 * n_qk + np.arange(n_v)])
    base = 2 * n_qk + n_v
    hr, dr = RET_HEADS, RET_QK_DIM
    half = dr // 2
    rot = np.zeros(hr * dr, np.int64)
    for part in range(2):
        for h in range(hr):
            rot[part * hr * half + h * half:part * hr * half + (h + 1) * half] = h * dr + part * half + np.arange(half)
    n_r = hr * dr
    n_rv = hr * RET_V_DIM
    perm_r = np.concatenate([base + rot, base + n_r + rot, base + 2 * n_r + np.arange(2 * n_rv)])
    return perm_a, perm_r


def kernel(x, p, attn_norm, w_in, lam_q1, lam_k1, lam_q2, lam_k2, subln_gain, w_out, rel_bias, ffn_norm,
           peer_query, peer_subkeys, peer_u, peer_v, ple_norm, ple_gate_w, ple_gate_b, ple_proj, final_norm):
    b, s, d = x.shape
    t = b * s
    depth = w_in.shape[0]
    perm_a, perm_r = _in_proj_perms()
    bias = _bias_tiles(rel_bias, ATT_TILE)
    n_diff = DIFF_HEADS * DIFF_V_DIM
    h = x.reshape(t, d)
    for layer in range(depth):
        lambda_init = 0.8 - 0.6 * math.exp(-0.3 * layer)
        wl = w_in[layer]
        wa = wl[:, perm_a].astype(BF16)
        wr = wl[:, perm_r].astype(BF16)
        pa, pr = _in_proj(h, attn_norm[layer][None, :], wa, wr)

        lamv = jnp.stack([lam_q1[layer], lam_k1[layer], lam_q2[layer], lam_k2[layer]]).astype(F32)
        diff_out = _diff_attn(pa.reshape(b, s, -1), bias, lamv, subln_gain[layer][None, :].astype(F32),
                              lambda_init)
        ret_out = _retention(pr.reshape(b, s, -1), s)

        wo = w_out[layer].astype(BF16)
        h, m, qry = _out_proj(h, diff_out.reshape(t, -1), ret_out.reshape(t, -1), wo[:n_diff], wo[n_diff:],
                              ffn_norm[layer][None, :], peer_query[layer].astype(BF16))

        subkeys = peer_subkeys[layer].reshape(2 * PEER_HEADS, PEER_N_KEYS, PEER_KEY_DIM).astype(BF16)
        idx_t, gate_t = _peer_topk(qry, subkeys)
        idx = idx_t.T
        gates = gate_t.T
        wts = _peer_u(idx, m.reshape(t, SUBLANES, LANES), gates, _pack_table(peer_u[layer]))
        h = _peer_v(idx, wts, h.reshape(t, SUBLANES, LANES), _pack_table(peer_v[layer])).reshape(t, d)

        h = _ple(h, p[layer].reshape(t, -1), ple_norm[layer][None, :], ple_gate_w[layer].astype(BF16),
                 ple_gate_b[layer][None, :], ple_proj[layer].astype(BF16), final_norm[None, :],
                 final=(layer == depth - 1))
    return h.reshape(b, s, d)
```

```python
import functools
import math

import numpy as np
import jax
import jax.numpy as jnp
from jax import lax
from jax.experimental import pallas as pl
from jax.experimental.pallas import tpu as pltpu

F32 = jnp.float32
BF16 = jnp.bfloat16
I32 = jnp.int32

DIFF_HEADS = 4
DIFF_V_DIM = 128
DIFF_QK_DIM = 64
RET_HEADS = 4
RET_V_DIM = 128
RET_QK_DIM = 64
RET_CHUNK = 128
ROPE_BASE = 10000.0
T5_BUCKETS = 32
T5_MAX_EXACT = 16
T5_MAX_DISTANCE = 128
PEER_HEADS = 8
PEER_N_KEYS = 128
PEER_TOPK = 16
PEER_KEY_DIM = 128
N_SEL = PEER_HEADS * PEER_TOPK

LANES = 128
SUBLANES = 8
VMEM_LIMIT = 48 * 1024 * 1024
VMEM_LIMIT_TABLE = 56 * 1024 * 1024

ROW_TILE = 256
ATT_TILE = 256
TOPK_TILE = 256
PEER_TILE = 64
PEER_GROUP = 2
PEER_STAGES = 4
WORDS_PER_ROW = 4
NEG_BIG = -1e30


def _cparams(sem, limit=VMEM_LIMIT):
    return pltpu.CompilerParams(dimension_semantics=sem, vmem_limit_bytes=limit)


def _dot(a, b):
    return jnp.dot(a, b, preferred_element_type=F32)


def _dot_nt(a, b):
    return lax.dot_general(a, b, (((1,), (1,)), ((), ())), preferred_element_type=F32)


def _rms(x, eps):
    return x * lax.rsqrt(jnp.mean(x * x, axis=-1, keepdims=True) + eps)


def _in_proj_kernel(x_ref, g_ref, wa_ref, wr_ref, oa_ref, or_ref):
    a = (_rms(x_ref[...], 1e-6) * g_ref[...]).astype(BF16)
    oa_ref[...] = _dot(a, wa_ref[...]).astype(BF16)
    or_ref[...] = _dot(a, wr_ref[...])


def _in_proj(x2, gain, wa, wr):
    t, d = x2.shape
    na, nr = wa.shape[1], wr.shape[1]
    return pl.pallas_call(
        _in_proj_kernel,
        grid=(t // ROW_TILE,),
        in_specs=[
            pl.BlockSpec((ROW_TILE, d), lambda i: (i, 0)),
            pl.BlockSpec((1, d), lambda i: (0, 0)),
            pl.BlockSpec((d, na), lambda i: (0, 0)),
            pl.BlockSpec((d, nr), lambda i: (0, 0)),
        ],
        out_specs=[
            pl.BlockSpec((ROW_TILE, na), lambda i: (i, 0)),
            pl.BlockSpec((ROW_TILE, nr), lambda i: (i, 0)),
        ],
        out_shape=[jax.ShapeDtypeStruct((t, na), BF16), jax.ShapeDtypeStruct((t, nr), F32)],
        compiler_params=_cparams(("parallel",)),
        name="in_proj",
    )(x2, gain, wa, wr)


def _diff_attn_kernel(q_ref, k_ref, v_ref, bias_ref, lam_ref, g_ref, o_ref, *, lambda_init):
    i = pl.program_id(2)
    tq = q_ref.shape[0]
    tk = tq
    q = q_ref[...] * jnp.asarray(DIFF_QK_DIM ** -0.5, BF16)
    lane = lax.broadcasted_iota(I32, q.shape, 1)
    zero = jnp.zeros_like(q)
    q1 = jnp.where(lane < DIFF_QK_DIM, q, zero)
    q2 = jnp.where(lane >= DIFF_QK_DIM, q, zero)

    def online(s, m, l, acc, v):
        mn = jnp.maximum(m, jnp.max(s, axis=-1, keepdims=True))
        p = jnp.exp(s - mn)
        alpha = jnp.exp(m - mn)
        l = alpha * l + jnp.sum(p, axis=-1, keepdims=True)
        acc = alpha * acc + _dot(p.astype(BF16), v)
        return mn, l, acc

    def body(j, carry):
        m1, l1, a1, m2, l2, a2 = carry
        off = pl.multiple_of(j * tk, tk)
        k = k_ref[pl.ds(off, tk), :]
        v = v_ref[pl.ds(off, tk), :]
        bias = bias_ref[jnp.minimum(i - j, 2)]
        s1 = _dot_nt(q1, k) + bias
        s2 = _dot_nt(q2, k) + bias
        m1, l1, a1 = online(s1, m1, l1, a1, v)
        m2, l2, a2 = online(s2, m2, l2, a2, v)
        return m1, l1, a1, m2, l2, a2

    mi = jnp.full((tq, 1), NEG_BIG, F32)
    li = jnp.zeros((tq, 1), F32)
    ai = jnp.zeros((tq, DIFF_V_DIM), F32)
    m1, l1, a1, m2, l2, a2 = lax.fori_loop(0, i + 1, body, (mi, li, ai, mi, li, ai))

    lv = lam_ref[...]
    lam = (jnp.exp(jnp.sum(lv[0:1] * lv[1:2], axis=-1, keepdims=True))
           - jnp.exp(jnp.sum(lv[2:3] * lv[3:4], axis=-1, keepdims=True)) + lambda_init)
    o = a1 / l1 - lam * (a2 / l2)
    o = _rms(o, 1e-5) * g_ref[...] * (1.0 - lambda_init)
    o_ref[...] = o.astype(o_ref.dtype)


def _t5_bucket(n):
    exact = n < T5_MAX_EXACT
    nf = jnp.maximum(n, 1).astype(F32)
    large = T5_MAX_EXACT + (jnp.log(nf / T5_MAX_EXACT)
                            / math.log(T5_MAX_DISTANCE / T5_MAX_EXACT)
                            * (T5_BUCKETS - T5_MAX_EXACT)).astype(I32)
    large = jnp.minimum(large, T5_BUCKETS - 1)
    return jnp.where(exact, n, large)


def _bias_tiles(rel_bias, t):
    assert t >= T5_MAX_DISTANCE
    r = jnp.arange(t, dtype=I32)[:, None]
    c = jnp.arange(t, dtype=I32)[None, :]
    d0 = r - c
    diag = jnp.where((d0 >= 0)[..., None], rel_bias[_t5_bucket(jnp.maximum(d0, 0))].astype(F32), NEG_BIG)
    near = rel_bias[_t5_bucket(d0 + t)].astype(F32)
    far = jnp.broadcast_to(rel_bias[_t5_bucket(jnp.full((1, 1), 2 * t, I32))].astype(F32), near.shape)
    return jnp.transpose(jnp.stack([diag, near, far], axis=0), (3, 0, 1, 2))


def _diff_attn(qkv, bias, lamv, gain, lambda_init):
    b, s, _ = qkv.shape
    h = DIFF_HEADS
    t = ATT_TILE
    kern = functools.partial(_diff_attn_kernel, lambda_init=lambda_init)
    return pl.pallas_call(
        kern,
        grid=(b, h, s // t),
        in_specs=[
            pl.BlockSpec((None, t, LANES), lambda bi, hi, i: (bi, i, hi)),
            pl.BlockSpec((None, s, LANES), lambda bi, hi, i: (bi, 0, h + hi)),
            pl.BlockSpec((None, s, LANES), lambda bi, hi, i: (bi, 0, 2 * h + hi)),
            pl.BlockSpec((None, 3, t, t), lambda bi, hi, i: (hi, 0, 0, 0)),
            pl.BlockSpec((4, DIFF_QK_DIM), lambda bi, hi, i: (0, 0)),
            pl.BlockSpec((1, DIFF_V_DIM), lambda bi, hi, i: (0, 0)),
        ],
        out_specs=pl.BlockSpec((None, t, LANES), lambda bi, hi, i: (bi, i, hi)),
        out_shape=jax.ShapeDtypeStruct((b, s, h * DIFF_V_DIM), BF16),
        compiler_params=_cparams(("parallel", "parallel", "arbitrary")),
        name="diff_attn",
    )(qkv, qkv, qkv, bias, lamv, gain)


def _retention_kernel(q_ref, k_ref, v_ref, g_ref, cos_ref, sin_ref, decay_ref, zeta_ref, xi_ref,
                      cd_ref, o_ref, state_ref):
    @pl.when(pl.program_id(1) == 0)
    def _():
        state_ref[...] = jnp.zeros_like(state_ref)

    cos = cos_ref[...]
    sin = sin_ref[...]

    def rotate(x):
        x1 = x[:, :LANES]
        x2 = x[:, LANES:]
        return jnp.concatenate([x1 * cos - x2 * sin, x1 * sin + x2 * cos], axis=1)

    qr = rotate(q_ref[...])
    kr = rotate(k_ref[...]) * (RET_QK_DIM ** -0.5)
    qx = qr * xi_ref[...]
    kz = kr * zeta_ref[...]
    lane = lax.broadcasted_iota(I32, qr.shape, 1)
    head_of_lane = (lane & (LANES - 1)) >> 5
    zero = jnp.zeros_like(qr)
    for h in range(RET_HEADS):
        msk = head_of_lane == h
        qh = jnp.where(msk, qr, zero).astype(BF16)
        kh = jnp.where(msk, kr, zero).astype(BF16)
        vh = v_ref[:, h * RET_V_DIM:(h + 1) * RET_V_DIM].astype(BF16)
        scores = _dot_nt(qh, kh) * decay_ref[h]
        inner = _dot(scores.astype(BF16), vh)
        st = state_ref[h]
        cross = _dot(jnp.where(msk, qx, zero).astype(BF16), st.astype(BF16))
        kzh = jnp.where(msk, kz, zero).astype(BF16)
        kv = lax.dot_general(kzh, vh, (((0,), (0,)), ((), ())), preferred_element_type=F32)
        state_ref[h] = cd_ref[h] * st + kv
        r = _rms(inner + cross, 1e-6)
        g = g_ref[:, h * RET_V_DIM:(h + 1) * RET_V_DIM]
        silu = g * (1.0 / (1.0 + jnp.exp(-g)))
        o_ref[:, h * RET_V_DIM:(h + 1) * RET_V_DIM] = (r * silu).astype(o_ref.dtype)


def _retention(pr, s):
    b = pr.shape[0]
    c = RET_CHUNK
    hh = RET_HEADS
    half = RET_QK_DIM // 2
    gamma = 1.0 - jnp.exp2(-5.0 - jnp.arange(hh, dtype=F32))
    lg = jnp.log(gamma)
    i = jnp.arange(c, dtype=F32)
    rel = i[:, None] - i[None, :]
    decay = jnp.where(rel >= 0, jnp.exp(jnp.maximum(rel, 0.0)[None] * lg[:, None, None]), 0.0)
    zeta = jnp.exp((c - 1 - i)[None, :] * lg[:, None])
    xi = jnp.exp((i + 1)[None, :] * lg[:, None])
    cd = jnp.exp(c * lg)
    head_of_lane = (np.arange(2 * LANES) % LANES) // half
    zeta_tab = zeta.T[:, head_of_lane]
    xi_tab = xi.T[:, head_of_lane]
    pos = jnp.arange(s, dtype=F32)
    freqs = 1.0 / (ROPE_BASE ** (jnp.arange(half, dtype=F32) / half))
    ang = pos[:, None] * freqs[None, :]
    cos_tab = jnp.tile(jnp.cos(ang), (1, hh))
    sin_tab = jnp.tile(jnp.sin(ang), (1, hh))
    w = hh * RET_V_DIM
    return pl.pallas_call(
        _retention_kernel,
        grid=(b, s // c),
        in_specs=[
            pl.BlockSpec((None, c, 2 * LANES), lambda bi, n: (bi, n, 0)),
            pl.BlockSpec((None, c, 2 * LANES), lambda bi, n: (bi, n, 1)),
            pl.BlockSpec((None, c, w), lambda bi, n: (bi, n, 1)),
            pl.BlockSpec((None, c, w), lambda bi, n: (bi, n, 2)),
            pl.BlockSpec((c, LANES), lambda bi, n: (n, 0)),
            pl.BlockSpec((c, LANES), lambda bi, n: (n, 0)),
            pl.BlockSpec((hh, c, c), lambda bi, n: (0, 0, 0)),
            pl.BlockSpec((c, 2 * LANES), lambda bi, n: (0, 0)),
            pl.BlockSpec((c, 2 * LANES), lambda bi, n: (0, 0)),
            pl.BlockSpec(memory_space=pltpu.SMEM),
        ],
        out_specs=pl.BlockSpec((None, c, w), lambda bi, n: (bi, n, 0)),
        out_shape=jax.ShapeDtypeStruct((b, s, w), BF16),
        scratch_shapes=[pltpu.VMEM((hh, 2 * LANES, RET_V_DIM), F32)],
        compiler_params=_cparams(("parallel", "arbitrary")),
        name="retention",
    )(pr, pr, pr, pr, cos_tab, sin_tab, decay, zeta_tab, xi_tab, cd)


def _out_proj_kernel(x_ref, d_ref, r_ref, wa_ref, wb_ref, g_ref, wq_ref, h_ref, m_ref, q_ref):
    h = x_ref[...] + _dot(d_ref[...], wa_ref[...]) + _dot(r_ref[...], wb_ref[...])
    h_ref[...] = h
    m = _rms(h, 1e-6) * g_ref[...]
    m_ref[...] = m
    q_ref[...] = _dot(m.astype(BF16), wq_ref[...]).astype(BF16)


def _out_proj(x2, d2, r2, wa, wb, gain, wq):
    t, d = x2.shape
    nq = wq.shape[1]
    kd, kr = d2.shape[1], r2.shape[1]
    row = lambda i: (i, 0)
    full = lambda i: (0, 0)
    return pl.pallas_call(
        _out_proj_kernel,
        grid=(t // ROW_TILE,),
        in_specs=[
            pl.BlockSpec((ROW_TILE, d), row),
            pl.BlockSpec((ROW_TILE, kd), row),
            pl.BlockSpec((ROW_TILE, kr), row),
            pl.BlockSpec((kd, d), full),
            pl.BlockSpec((kr, d), full),
            pl.BlockSpec((1, d), full),
            pl.BlockSpec((d, nq), full),
        ],
        out_specs=[
            pl.BlockSpec((ROW_TILE, d), row),
            pl.BlockSpec((ROW_TILE, d), row),
            pl.BlockSpec((ROW_TILE, nq), row),
        ],
        out_shape=[jax.ShapeDtypeStruct((t, d), F32), jax.ShapeDtypeStruct((t, d), F32),
                   jax.ShapeDtypeStruct((t, nq), BF16)],
        compiler_params=_cparams(("parallel",)),
        name="out_proj",
    )(x2, d2, r2, wa, wb, gain, wq)


def _topk_rows(s, k, fill):
    rows = lax.broadcasted_iota(I32, s.shape, 0)
    vals, ids = [], []
    for _ in range(k):
        mx = jnp.max(s, axis=0, keepdims=True)
        am = jnp.min(jnp.where(s == mx, rows, s.shape[0]), axis=0, keepdims=True)
        vals.append(mx)
        ids.append(am)
        s = jnp.where(rows == am, fill, s)
    return vals, ids


def _peer_topk_kernel(q_ref, sk_ref, idx_ref, gate_ref):
    k = PEER_TOPK
    neg_inf = jnp.asarray(-jnp.inf, F32)
    half_vals, half_ids = [], []
    for c in range(2):
        qc = q_ref[:, c * PEER_KEY_DIM:(c + 1) * PEER_KEY_DIM]
        s = _dot_nt(sk_ref[c], qc)
        vals, ids = _topk_rows(s, k, neg_inf)
        half_vals.append(jnp.concatenate(vals, axis=0))
        half_ids.append(jnp.concatenate(ids, axis=0))
    v1, v2 = half_vals
    i1, i2 = half_ids
    cand = jnp.concatenate([v1[a:a + 1] + v2 for a in range(k)], axis=0)
    cidx = jnp.concatenate([i1[a:a + 1] * PEER_N_KEYS + i2 for a in range(k)], axis=0)
    rows = lax.broadcasted_iota(I32, cand.shape, 0)
    top_s, top_e = [], []
    for _ in range(k):
        mx = jnp.max(cand, axis=0, keepdims=True)
        pos = jnp.min(jnp.where(cand == mx, rows, k * k), axis=0, keepdims=True)
        hit = rows == pos
        top_s.append(mx)
        top_e.append(jnp.max(jnp.where(hit, cidx, -1), axis=0, keepdims=True))
        cand = jnp.where(hit, neg_inf, cand)
    ts = jnp.concatenate(top_s, axis=0)
    ex = jnp.exp(ts - ts[0:1])
    gate_ref[...] = ex / jnp.sum(ex, axis=0, keepdims=True)
    idx_ref[...] = jnp.concatenate(top_e, axis=0) * WORDS_PER_ROW


def _peer_topk(qry, subkeys):
    t = qry.shape[0]
    tb = TOPK_TILE
    k = PEER_TOPK
    return pl.pallas_call(
        _peer_topk_kernel,
        grid=(t // tb, PEER_HEADS),
        in_specs=[
            pl.BlockSpec((tb, 2 * PEER_KEY_DIM), lambda i, h: (i, h)),
            pl.BlockSpec((2, PEER_N_KEYS, PEER_KEY_DIM), lambda i, h: (h, 0, 0)),
        ],
        out_specs=[
            pl.BlockSpec((k, tb), lambda i, h: (h, i)),
            pl.BlockSpec((k, tb), lambda i, h: (h, i)),
        ],
        out_shape=[jax.ShapeDtypeStruct((N_SEL, t), I32), jax.ShapeDtypeStruct((N_SEL, t), F32)],
        compiler_params=_cparams(("parallel", "parallel")),
        name="peer_topk",
    )(qry, subkeys)


def _pack_table(w):
    n, d = w.shape
    wb = lax.bitcast_convert_type(w.astype(BF16), jnp.uint16).astype(jnp.uint32)
    words = (wb[:, d // 2:] << 16) | wb[:, :d // 2]
    return lax.bitcast_convert_type(words, I32).reshape(n * WORDS_PER_ROW, LANES)


def _split_bf16(x):
    hi = x.astype(BF16)
    lo = (x - hi.astype(F32)).astype(BF16)
    return hi, lo


def _feature_row_mask(width):
    r = lax.broadcasted_iota(I32, (SUBLANES, width), 0)
    k = lax.broadcasted_iota(I32, (SUBLANES, width), 1)
    return (k & 7) == 2 * (r & 3) + (r >> 2)


def _expand_matrix():
    e = np.arange(N_SEL)[:, None]
    k = np.arange(SUBLANES * N_SEL)[None, :]
    return jnp.asarray((k // SUBLANES == e), BF16)


def _gather_group(idx_ref, t, tab_ref, tiles_ref, experts):
    ids = [idx_ref.at[t + k] for k in range(PEER_GROUP)]
    for e in experts:
        for k in range(PEER_GROUP):
            row = pl.multiple_of(ids[k][e], WORDS_PER_ROW)
            tiles_ref[k, pl.ds(e * WORDS_PER_ROW, WORDS_PER_ROW), :] = tab_ref[pl.ds(row, WORDS_PER_ROW), :]


def _token_pipeline(tb, idx_ref, tab_ref, stages, tiles_a, tiles_b):
    g = PEER_GROUP
    _gather_group(idx_ref, 0, tab_ref, tiles_a, range(N_SEL))

    def phase(t0, src, dst):
        tn = jnp.minimum(t0 + g, tb - g)
        pieces = list(stages(t0, src))
        per = N_SEL // PEER_STAGES
        for s in range(PEER_STAGES):
            pieces[s]()
            _gather_group(idx_ref, tn, tab_ref, dst, range(s * per, (s + 1) * per))
        for piece in pieces[PEER_STAGES:]:
            piece()

    def body(i, carry):
        t0 = pl.multiple_of(i * g, g)
        pl.when((i & 1) == 0)(lambda: phase(t0, tiles_a, tiles_b))
        pl.when((i & 1) == 1)(lambda: phase(t0, tiles_b, tiles_a))
        return carry

    lax.fori_loop(0, tb // g, body, 0)


def _tile_chunk(tiles_ref, k, c):
    rows = WORDS_PER_ROW * N_SEL // PEER_STAGES
    return pltpu.bitcast(tiles_ref[k, pl.ds(c * rows, rows), :], BF16)


def _peer_u_kernel(idx_ref, m_ref, gate_ref, tab_ref, ex_ref, ext_ref, w8_ref, q_ref, tiles_a, tiles_b):
    tb = m_ref.shape[0]
    cw = SUBLANES * N_SEL // PEER_STAGES
    mask = _feature_row_mask(cw)

    def stages(t0, src):
        x16 = []

        def prep_and_chunk0():
            for k in range(PEER_GROUP):
                xh, xl = _split_bf16(m_ref[t0 + k])
                x16.append(jnp.concatenate([xh, xl], axis=0))
            chunk(0)

        def chunk(c):
            for k in range(PEER_GROUP):
                p = _dot_nt(x16[k], _tile_chunk(src, k, c))
                row = pl.multiple_of((t0 + k) * SUBLANES, SUBLANES)
                q_ref[pl.ds(row, SUBLANES), c * cw:(c + 1) * cw] = jnp.where(mask, p[:SUBLANES] + p[SUBLANES:], 0.0)

        return [prep_and_chunk0] + [functools.partial(chunk, c) for c in range(1, PEER_STAGES)]

    _token_pipeline(tb, idx_ref, tab_ref, stages, tiles_a, tiles_b)

    qh, ql = _split_bf16(q_ref[...])
    r = _dot(qh, ext_ref[...]) + _dot(ql, ext_ref[...])
    a = jnp.sum(r.reshape(tb, SUBLANES, N_SEL), axis=1)
    gelu = 0.5 * a * (1.0 + lax.erf(a * (2.0 ** -0.5)))
    wh, wl = _split_bf16(gelu * gate_ref[...])
    w8_ref[...] = _dot(wh, ex_ref[...]) + _dot(wl, ex_ref[...])


def _peer_u(idx, m3, gates, table, ex, ext):
    t = idx.shape[0]
    tb = PEER_TILE
    wide = SUBLANES * N_SEL
    tiles = pltpu.VMEM((PEER_GROUP, WORDS_PER_ROW * N_SEL, LANES), I32)
    return pl.pallas_call(
        _peer_u_kernel,
        grid=(t // tb,),
        in_specs=[
            pl.BlockSpec((tb, N_SEL), lambda i: (i, 0), memory_space=pltpu.SMEM),
            pl.BlockSpec((tb, SUBLANES, LANES), lambda i: (i, 0, 0)),
            pl.BlockSpec((tb, N_SEL), lambda i: (i, 0)),
            pl.BlockSpec(table.shape, lambda i: (0, 0)),
            pl.BlockSpec((N_SEL, wide), lambda i: (0, 0)),
            pl.BlockSpec((wide, N_SEL), lambda i: (0, 0)),
        ],
        out_specs=pl.BlockSpec((tb, wide), lambda i: (i, 0)),
        out_shape=jax.ShapeDtypeStruct((t, wide), F32),
        scratch_shapes=[pltpu.VMEM((tb * SUBLANES, wide), F32), tiles, tiles],
        compiler_params=_cparams(("arbitrary",), VMEM_LIMIT_TABLE),
        name="peer_u",
    )(idx, m3, gates, table, ex, ext)


def _peer_v_kernel(idx_ref, w8_ref, h_ref, tab_ref, o_ref, tiles_a, tiles_b):
    tb = h_ref.shape[0]
    wide = SUBLANES * N_SEL
    cw = wide // PEER_STAGES
    mask = _feature_row_mask(wide)

    def stages(t0, src):
        w16 = []
        parts = [[] for _ in range(PEER_GROUP)]

        def prep_and_chunk0():
            for k in range(PEER_GROUP):
                wrow = jnp.broadcast_to(w8_ref[pl.ds(t0 + k, 1), :], mask.shape)
                wh, wl = _split_bf16(jnp.where(mask, wrow, 0.0))
                w16.append(jnp.concatenate([wh, wl], axis=0))
            chunk(0)

        def chunk(c):
            for k in range(PEER_GROUP):
                parts[k].append(_dot(w16[k][:, c * cw:(c + 1) * cw], _tile_chunk(src, k, c)))

        def finish():
            for k in range(PEER_GROUP):
                o16 = (parts[k][0] + parts[k][1]) + (parts[k][2] + parts[k][3])
                o_ref[t0 + k] = h_ref[t0 + k] + (o16[:SUBLANES] + o16[SUBLANES:])

        return [prep_and_chunk0] + [functools.partial(chunk, c) for c in range(1, PEER_STAGES)] + [finish]

    _token_pipeline(tb, idx_ref, tab_ref, stages, tiles_a, tiles_b)


def _peer_v(idx, w8, h3, table):
    t = idx.shape[0]
    tb = PEER_TILE
    wide = SUBLANES * N_SEL
    tiles = pltpu.VMEM((PEER_GROUP, WORDS_PER_ROW * N_SEL, LANES), I32)
    return pl.pallas_call(
        _peer_v_kernel,
        grid=(t // tb,),
        in_specs=[
            pl.BlockSpec((tb, N_SEL), lambda i: (i, 0), memory_space=pltpu.SMEM),
            pl.BlockSpec((tb, wide), lambda i: (i, 0)),
            pl.BlockSpec((tb, SUBLANES, LANES), lambda i: (i, 0, 0)),
            pl.BlockSpec(table.shape, lambda i: (0, 0)),
        ],
        out_specs=pl.BlockSpec((tb, SUBLANES, LANES), lambda i: (i, 0, 0)),
        out_shape=jax.ShapeDtypeStruct(h3.shape, F32),
        scratch_shapes=[tiles, tiles],
        compiler_params=_cparams(("arbitrary",), VMEM_LIMIT_TABLE),
        name="peer_v",
    )(idx, w8, h3, table)


def _ple_kernel(h_ref, p_ref, gn_ref, wg_ref, bg_ref, wp_ref, fn_ref, o_ref, *, final):
    h = h_ref[...]
    n = (_rms(h, 1e-6) * gn_ref[...]).astype(BF16)
    z = _dot(n, wg_ref[...]) + bg_ref[...]
    gate = 1.0 / (1.0 + jnp.exp(-z))
    h = h + gate * _dot(p_ref[...].astype(BF16), wp_ref[...])
    if final:
        h = _rms(h, 1e-6) * fn_ref[...]
    o_ref[...] = h


def _ple(h2, p2, gn, wg, bg, wp, fn, final):
    t, d = h2.shape
    pd = p2.shape[1]
    row = lambda i: (i, 0)
    full = lambda i: (0, 0)
    return pl.pallas_call(
        functools.partial(_ple_kernel, final=final),
        grid=(t // ROW_TILE,),
        in_specs=[
            pl.BlockSpec((ROW_TILE, d), row),
            pl.BlockSpec((ROW_TILE, pd), row),
            pl.BlockSpec((1, d), full),
            pl.BlockSpec((d, d), full),
            pl.BlockSpec((1, d), full),
            pl.BlockSpec((pd, d), full),
            pl.BlockSpec((1, d), full),
        ],
        out_specs=pl.BlockSpec((ROW_TILE, d), row),
        out_shape=jax.ShapeDtypeStruct((t, d), F32),
        compiler_params=_cparams(("parallel",)),
        name="ple_final",
    )(h2, p2, gn, wg, bg, wp, fn)


def _permute_w_in(w):
    d = w.shape[0]
    hq, dq = DIFF_HEADS, DIFF_QK_DIM
    n_qk = 2 * hq * dq
    n_v = hq * DIFF_V_DIM
    swap_qk = lambda c: c.reshape(d, 2, hq, dq).transpose(0, 2, 1, 3).reshape(d, n_qk)
    wa = jnp.concatenate([swap_qk(w[:, :n_qk]), swap_qk(w[:, n_qk:2 * n_qk]), w[:, 2 * n_qk:2 * n_qk + n_v]],
                         axis=1)
    base = 2 * n_qk + n_v
    hr, dr = RET_HEADS, RET_QK_DIM
    n_r = hr * dr
    swap_r = lambda c: c.reshape(d, hr, 2, dr // 2).transpose(0, 2, 1, 3).reshape(d, n_r)
    wr = jnp.concatenate([swap_r(w[:, base:base + n_r]), swap_r(w[:, base + n_r:base + 2 * n_r]),
                          w[:, base + 2 * n_r:]], axis=1)
    return wa.astype(BF16), wr.astype(BF16)


def kernel(x, p, attn_norm, w_in, lam_q1, lam_k1, lam_q2, lam_k2, subln_gain, w_out, rel_bias, ffn_norm,
           peer_query, peer_subkeys, peer_u, peer_v, ple_norm, ple_gate_w, ple_gate_b, ple_proj, final_norm):
    b, s, d = x.shape
    t = b * s
    depth = w_in.shape[0]
    bias = _bias_tiles(rel_bias, ATT_TILE)
    n_diff = DIFF_HEADS * DIFF_V_DIM
    ex = _expand_matrix()
    h = x.reshape(t, d)
    for layer in range(depth):
        lambda_init = 0.8 - 0.6 * math.exp(-0.3 * layer)
        wa, wr = _permute_w_in(w_in[layer])
        pa, pr = _in_proj(h, attn_norm[layer][None, :], wa, wr)

        lamv = jnp.stack([lam_q1[layer], lam_k1[layer], lam_q2[layer], lam_k2[layer]]).astype(F32)
        diff_out = _diff_attn(pa.reshape(b, s, -1), bias, lamv, subln_gain[layer][None, :].astype(F32),
                              lambda_init)
        ret_out = _retention(pr.reshape(b, s, -1), s)

        wo = w_out[layer].astype(BF16)
        h, m, qry = _out_proj(h, diff_out.reshape(t, -1), ret_out.reshape(t, -1), wo[:n_diff], wo[n_diff:],
                              ffn_norm[layer][None, :], peer_query[layer].astype(BF16))

        subkeys = peer_subkeys[layer].reshape(2 * PEER_HEADS, PEER_N_KEYS, PEER_KEY_DIM).astype(BF16)
        idx_t, gate_t = _peer_topk(qry, subkeys)
        idx = idx_t.T
        gates = gate_t.T
        w8 = _peer_u(idx, m.reshape(t, SUBLANES, LANES), gates, _pack_table(peer_u[layer]), ex, ex.T)
        h = _peer_v(idx, w8, h.reshape(t, SUBLANES, LANES), _pack_table(peer_v[layer])).reshape(t, d)

        h = _ple(h, p[layer].reshape(t, -1), ple_norm[layer][None, :], ple_gate_w[layer].astype(BF16),
                 ple_gate_b[layer][None, :], ple_proj[layer].astype(BF16), final_norm[None, :],
                 final=(layer == depth - 1))
    return h.reshape(b, s, d)
```

```python
import functools
import math

import numpy as np
import jax
import jax.numpy as jnp
from jax import lax
from jax.experimental import pallas as pl
from jax.experimental.pallas import tpu as pltpu

F32 = jnp.float32
BF16 = jnp.bfloat16
I32 = jnp.int32

DIFF_HEADS = 4
DIFF_V_DIM = 128
DIFF_QK_DIM = 64
RET_HEADS = 4
RET_V_DIM = 128
RET_QK_DIM = 64
RET_CHUNK = 128
ROPE_BASE = 10000.0
T5_BUCKETS = 32
T5_MAX_EXACT = 16
T5_MAX_DISTANCE = 128
PEER_HEADS = 8
PEER_N_KEYS = 128
PEER_TOPK = 16
PEER_KEY_DIM = 128
N_SEL = PEER_HEADS * PEER_TOPK

LANES = 128
SUBLANES = 8
VMEM_LIMIT = 48 * 1024 * 1024
VMEM_LIMIT_TABLE = 56 * 1024 * 1024

ROW_TILE = 256
ATT_TILE = 256
TOPK_TILE = 256
PEER_TILE = 64
PEER_GROUP = 4
PEER_STAGES = 4
WORDS_PER_ROW = 4
NEG_BIG = -1e30


def _cparams(sem, limit=VMEM_LIMIT):
    return pltpu.CompilerParams(dimension_semantics=sem, vmem_limit_bytes=limit)


def _dot(a, b):
    return jnp.dot(a, b, preferred_element_type=F32)


def _dot_nt(a, b):
    return lax.dot_general(a, b, (((1,), (1,)), ((), ())), preferred_element_type=F32)


def _rms(x, eps):
    return x * lax.rsqrt(jnp.mean(x * x, axis=-1, keepdims=True) + eps)


def _in_proj_kernel(x_ref, g_ref, wa_ref, wr_ref, oa_ref, or_ref):
    a = (_rms(x_ref[...], 1e-6) * g_ref[...]).astype(BF16)
    oa_ref[...] = _dot(a, wa_ref[...]).astype(BF16)
    or_ref[...] = _dot(a, wr_ref[...])


def _in_proj(x2, gain, wa, wr):
    t, d = x2.shape
    na, nr = wa.shape[1], wr.shape[1]
    return pl.pallas_call(
        _in_proj_kernel,
        grid=(t // ROW_TILE,),
        in_specs=[
            pl.BlockSpec((ROW_TILE, d), lambda i: (i, 0)),
            pl.BlockSpec((1, d), lambda i: (0, 0)),
            pl.BlockSpec((d, na), lambda i: (0, 0)),
            pl.BlockSpec((d, nr), lambda i: (0, 0)),
        ],
        out_specs=[
            pl.BlockSpec((ROW_TILE, na), lambda i: (i, 0)),
            pl.BlockSpec((ROW_TILE, nr), lambda i: (i, 0)),
        ],
        out_shape=[jax.ShapeDtypeStruct((t, na), BF16), jax.ShapeDtypeStruct((t, nr), F32)],
        compiler_params=_cparams(("parallel",)),
        name="in_proj",
    )(x2, gain, wa, wr)


def _diff_attn_kernel(q_ref, k_ref, v_ref, bias_ref, lam_ref, g_ref, o_ref, *, lambda_init):
    i = pl.program_id(2)
    tq = q_ref.shape[0]
    tk = tq
    q = q_ref[...] * jnp.asarray(DIFF_QK_DIM ** -0.5, BF16)
    lane = lax.broadcasted_iota(I32, q.shape, 1)
    zero = jnp.zeros_like(q)
    q1 = jnp.where(lane < DIFF_QK_DIM, q, zero)
    q2 = jnp.where(lane >= DIFF_QK_DIM, q, zero)

    def online(s, m, l, acc, v):
        mn = jnp.maximum(m, jnp.max(s, axis=-1, keepdims=True))
        p = jnp.exp(s - mn)
        alpha = jnp.exp(m - mn)
        l = alpha * l + jnp.sum(p, axis=-1, keepdims=True)
        acc = alpha * acc + _dot(p.astype(BF16), v)
        return mn, l, acc

    def body(j, carry):
        m1, l1, a1, m2, l2, a2 = carry
        off = pl.multiple_of(j * tk, tk)
        k = k_ref[pl.ds(off, tk), :]
        v = v_ref[pl.ds(off, tk), :]
        bias = bias_ref[jnp.minimum(i - j, 2)]
        s1 = _dot_nt(q1, k) + bias
        s2 = _dot_nt(q2, k) + bias
        m1, l1, a1 = online(s1, m1, l1, a1, v)
        m2, l2, a2 = online(s2, m2, l2, a2, v)
        return m1, l1, a1, m2, l2, a2

    mi = jnp.full((tq, 1), NEG_BIG, F32)
    li = jnp.zeros((tq, 1), F32)
    ai = jnp.zeros((tq, DIFF_V_DIM), F32)
    m1, l1, a1, m2, l2, a2 = lax.fori_loop(0, i + 1, body, (mi, li, ai, mi, li, ai))

    lv = lam_ref[...]
    lam = (jnp.exp(jnp.sum(lv[0:1] * lv[1:2], axis=-1, keepdims=True))
           - jnp.exp(jnp.sum(lv[2:3] * lv[3:4], axis=-1, keepdims=True)) + lambda_init)
    o = a1 / l1 - lam * (a2 / l2)
    o = _rms(o, 1e-5) * g_ref[...] * (1.0 - lambda_init)
    o_ref[...] = o.astype(o_ref.dtype)


def _t5_bucket(n):
    exact = n < T5_MAX_EXACT
    nf = jnp.maximum(n, 1).astype(F32)
    large = T5_MAX_EXACT + (jnp.log(nf / T5_MAX_EXACT)
                            / math.log(T5_MAX_DISTANCE / T5_MAX_EXACT)
                            * (T5_BUCKETS - T5_MAX_EXACT)).astype(I32)
    large = jnp.minimum(large, T5_BUCKETS - 1)
    return jnp.where(exact, n, large)


def _bias_tiles(rel_bias, t):
    assert t >= T5_MAX_DISTANCE
    def lookup(dist):
        onehot = (_t5_bucket(dist)[..., None] == jnp.arange(T5_BUCKETS, dtype=I32)).astype(F32)
        return jnp.einsum("rcb,bh->hrc", onehot, rel_bias.astype(F32), precision=lax.Precision.HIGHEST)

    r = jnp.arange(t, dtype=I32)[:, None]
    c = jnp.arange(t, dtype=I32)[None, :]
    d0 = r - c
    diag = jnp.where((d0 >= 0)[None], lookup(jnp.maximum(d0, 0)), NEG_BIG)
    near = lookup(d0 + t)
    far = lookup(jnp.full((t, t), 2 * t, I32))
    return jnp.stack([diag, near, far], axis=1)


def _diff_attn(qkv, bias, lamv, gain, lambda_init):
    b, s, _ = qkv.shape
    h = DIFF_HEADS
    t = ATT_TILE
    kern = functools.partial(_diff_attn_kernel, lambda_init=lambda_init)
    return pl.pallas_call(
        kern,
        grid=(b, h, s // t),
        in_specs=[
            pl.BlockSpec((None, t, LANES), lambda bi, hi, i: (bi, i, hi)),
            pl.BlockSpec((None, s, LANES), lambda bi, hi, i: (bi, 0, h + hi)),
            pl.BlockSpec((None, s, LANES), lambda bi, hi, i: (bi, 0, 2 * h + hi)),
            pl.BlockSpec((None, 3, t, t), lambda bi, hi, i: (hi, 0, 0, 0)),
            pl.BlockSpec((4, DIFF_QK_DIM), lambda bi, hi, i: (0, 0)),
            pl.BlockSpec((1, DIFF_V_DIM), lambda bi, hi, i: (0, 0)),
        ],
        out_specs=pl.BlockSpec((None, t, LANES), lambda bi, hi, i: (bi, i, hi)),
        out_shape=jax.ShapeDtypeStruct((b, s, h * DIFF_V_DIM), BF16),
        compiler_params=_cparams(("parallel", "parallel", "arbitrary")),
        name="diff_attn",
    )(qkv, qkv, qkv, bias, lamv, gain)


def _retention_kernel(q_ref, k_ref, v_ref, g_ref, cos_ref, sin_ref, decay_ref, zeta_ref, xi_ref,
                      cd_ref, o_ref, state_ref):
    @pl.when(pl.program_id(1) == 0)
    def _():
        state_ref[...] = jnp.zeros_like(state_ref)

    cos = cos_ref[...]
    sin = sin_ref[...]

    def rotate(x):
        x1 = x[:, :LANES]
        x2 = x[:, LANES:]
        return jnp.concatenate([x1 * cos - x2 * sin, x1 * sin + x2 * cos], axis=1)

    qr = rotate(q_ref[...])
    kr = rotate(k_ref[...]) * (RET_QK_DIM ** -0.5)
    qx = qr * xi_ref[...]
    kz = kr * zeta_ref[...]
    lane = lax.broadcasted_iota(I32, qr.shape, 1)
    head_of_lane = (lane & (LANES - 1)) >> 5
    zero = jnp.zeros_like(qr)
    for h in range(RET_HEADS):
        msk = head_of_lane == h
        qh = jnp.where(msk, qr, zero).astype(BF16)
        kh = jnp.where(msk, kr, zero).astype(BF16)
        vh = v_ref[:, h * RET_V_DIM:(h + 1) * RET_V_DIM].astype(BF16)
        scores = _dot_nt(qh, kh) * decay_ref[h]
        inner = _dot(scores.astype(BF16), vh)
        st = state_ref[h]
        cross = _dot(jnp.where(msk, qx, zero).astype(BF16), st.astype(BF16))
        kzh = jnp.where(msk, kz, zero).astype(BF16)
        kv = lax.dot_general(kzh, vh, (((0,), (0,)), ((), ())), preferred_element_type=F32)
        state_ref[h] = cd_ref[h] * st + kv
        r = _rms(inner + cross, 1e-6)
        g = g_ref[:, h * RET_V_DIM:(h + 1) * RET_V_DIM]
        silu = g * (1.0 / (1.0 + jnp.exp(-g)))
        o_ref[:, h * RET_V_DIM:(h + 1) * RET_V_DIM] = (r * silu).astype(o_ref.dtype)


def _retention(pr, s):
    b = pr.shape[0]
    c = RET_CHUNK
    hh = RET_HEADS
    half = RET_QK_DIM // 2
    gamma = 1.0 - jnp.exp2(-5.0 - jnp.arange(hh, dtype=F32))
    lg = jnp.log(gamma)
    i = jnp.arange(c, dtype=F32)
    rel = i[:, None] - i[None, :]
    decay = jnp.where(rel >= 0, jnp.exp(jnp.maximum(rel, 0.0)[None] * lg[:, None, None]), 0.0)
    zeta = jnp.exp((c - 1 - i)[None, :] * lg[:, None])
    xi = jnp.exp((i + 1)[None, :] * lg[:, None])
    cd = jnp.exp(c * lg)
    head_of_lane = (np.arange(2 * LANES) % LANES) // half
    zeta_tab = zeta.T[:, head_of_lane]
    xi_tab = xi.T[:, head_of_lane]
    pos = jnp.arange(s, dtype=F32)
    freqs = 1.0 / (ROPE_BASE ** (jnp.arange(half, dtype=F32) / half))
    ang = pos[:, None] * freqs[None, :]
    cos_tab = jnp.tile(jnp.cos(ang), (1, hh))
    sin_tab = jnp.tile(jnp.sin(ang), (1, hh))
    w = hh * RET_V_DIM
    return pl.pallas_call(
        _retention_kernel,
        grid=(b, s // c),
        in_specs=[
            pl.BlockSpec((None, c, 2 * LANES), lambda bi, n: (bi, n, 0)),
            pl.BlockSpec((None, c, 2 * LANES), lambda bi, n: (bi, n, 1)),
            pl.BlockSpec((None, c, w), lambda bi, n: (bi, n, 1)),
            pl.BlockSpec((None, c, w), lambda bi, n: (bi, n, 2)),
            pl.BlockSpec((c, LANES), lambda bi, n: (n, 0)),
            pl.BlockSpec((c, LANES), lambda bi, n: (n, 0)),
            pl.BlockSpec((hh, c, c), lambda bi, n: (0, 0, 0)),
            pl.BlockSpec((c, 2 * LANES), lambda bi, n: (0, 0)),
            pl.BlockSpec((c, 2 * LANES), lambda bi, n: (0, 0)),
            pl.BlockSpec(memory_space=pltpu.SMEM),
        ],
        out_specs=pl.BlockSpec((None, c, w), lambda bi, n: (bi, n, 0)),
        out_shape=jax.ShapeDtypeStruct((b, s, w), BF16),
        scratch_shapes=[pltpu.VMEM((hh, 2 * LANES, RET_V_DIM), F32)],
        compiler_params=_cparams(("parallel", "arbitrary")),
        name="retention",
    )(pr, pr, pr, pr, cos_tab, sin_tab, decay, zeta_tab, xi_tab, cd)


def _out_proj_kernel(x_ref, d_ref, r_ref, wa_ref, wb_ref, g_ref, wq_ref, h_ref, m_ref, q_ref):
    h = x_ref[...] + _dot(d_ref[...], wa_ref[...]) + _dot(r_ref[...], wb_ref[...])
    h_ref[...] = h
    m = _rms(h, 1e-6) * g_ref[...]
    m_ref[...] = m
    q_ref[...] = _dot(m.astype(BF16), wq_ref[...]).astype(BF16)


def _out_proj(x2, d2, r2, wa, wb, gain, wq):
    t, d = x2.shape
    nq = wq.shape[1]
    kd, kr = d2.shape[1], r2.shape[1]
    row = lambda i: (i, 0)
    full = lambda i: (0, 0)
    return pl.pallas_call(
        _out_proj_kernel,
        grid=(t // ROW_TILE,),
        in_specs=[
            pl.BlockSpec((ROW_TILE, d), row),
            pl.BlockSpec((ROW_TILE, kd), row),
            pl.BlockSpec((ROW_TILE, kr), row),
            pl.BlockSpec((kd, d), full),
            pl.BlockSpec((kr, d), full),
            pl.BlockSpec((1, d), full),
            pl.BlockSpec((d, nq), full),
        ],
        out_specs=[
            pl.BlockSpec((ROW_TILE, d), row),
            pl.BlockSpec((ROW_TILE, d), row),
            pl.BlockSpec((ROW_TILE, nq), row),
        ],
        out_shape=[jax.ShapeDtypeStruct((t, d), F32), jax.ShapeDtypeStruct((t, d), F32),
                   jax.ShapeDtypeStruct((t, nq), BF16)],
        compiler_params=_cparams(("parallel",)),
        name="out_proj",
    )(x2, d2, r2, wa, wb, gain, wq)


def _topk_rows(s, keys, k):
    neg_inf = jnp.asarray(-jnp.inf, F32)
    big = jnp.asarray(jnp.inf, F32)
    vals, picked = [], []
    for _ in range(k):
        mx = jnp.max(s, axis=0, keepdims=True)
        key = jnp.min(jnp.where(s == mx, keys, big), axis=0, keepdims=True)
        vals.append(mx)
        picked.append(key)
        s = jnp.where(keys == key, neg_inf, s)
    return vals, picked


def _staircase(v1, v2, i1, i2):
    k = PEER_TOPK
    tb = v1.shape[1]
    row = lax.broadcasted_iota(I32, (SUBLANES, tb), 0)
    neg_inf = jnp.asarray(-jnp.inf, F32)
    vals, keys = [], []

    def add(val, pos, expert, valid=None):
        key = (pos * (PEER_N_KEYS * PEER_N_KEYS) + expert).astype(F32)
        vals.append(val if valid is None else jnp.where(valid, val, neg_inf))
        keys.append(key)

    for a in range(SUBLANES):
        bmax = k // (a + 1) - 1
        for b0 in range(0, bmax + 1, SUBLANES):
            b = b0 + row
            add(v1[a:a + 1] + v2[b0:b0 + SUBLANES], a * k + b, i1[a:a + 1] * PEER_N_KEYS + i2[b0:b0 + SUBLANES],
                None if b0 + SUBLANES - 1 <= bmax else b <= bmax)
    add(v1[SUBLANES:] + v2[0:1], (SUBLANES + row) * k, i1[SUBLANES:] * PEER_N_KEYS + i2[0:1])
    return jnp.concatenate(vals, axis=0), jnp.concatenate(keys, axis=0)


def _peer_topk_kernel(q_ref, sk_ref, idx_ref, gate_ref):
    k = PEER_TOPK
    tb = q_ref.shape[0]
    key_ids = lax.broadcasted_iota(I32, (PEER_N_KEYS, tb), 0).astype(F32)
    half_vals, half_ids = [], []
    for c in range(2):
        qc = q_ref[:, c * PEER_KEY_DIM:(c + 1) * PEER_KEY_DIM]
        s = _dot_nt(sk_ref[c], qc)
        vals, ids = _topk_rows(s, key_ids, k)
        half_vals.append(jnp.concatenate(vals, axis=0))
        half_ids.append(jnp.concatenate(ids, axis=0).astype(I32))
    cand, cand_keys = _staircase(half_vals[0], half_vals[1], half_ids[0], half_ids[1])
    top_s, top_key = _topk_rows(cand, cand_keys, k)
    ts = jnp.concatenate(top_s, axis=0)
    ex = jnp.exp(ts - ts[0:1])
    gate_ref[...] = ex / jnp.sum(ex, axis=0, keepdims=True)
    expert = jnp.concatenate(top_key, axis=0).astype(I32) & (PEER_N_KEYS * PEER_N_KEYS - 1)
    idx_ref[...] = expert * WORDS_PER_ROW


def _peer_topk(qry, subkeys):
    t = qry.shape[0]
    tb = TOPK_TILE
    k = PEER_TOPK
    return pl.pallas_call(
        _peer_topk_kernel,
        grid=(t // tb, PEER_HEADS),
        in_specs=[
            pl.BlockSpec((tb, 2 * PEER_KEY_DIM), lambda i, h: (i, h)),
            pl.BlockSpec((2, PEER_N_KEYS, PEER_KEY_DIM), lambda i, h: (h, 0, 0)),
        ],
        out_specs=[
            pl.BlockSpec((k, tb), lambda i, h: (h, i)),
            pl.BlockSpec((k, tb), lambda i, h: (h, i)),
        ],
        out_shape=[jax.ShapeDtypeStruct((N_SEL, t), I32), jax.ShapeDtypeStruct((N_SEL, t), F32)],
        compiler_params=_cparams(("parallel", "parallel")),
        name="peer_topk",
    )(qry, subkeys)


def _pack_table(w):
    n, d = w.shape
    wb = lax.bitcast_convert_type(w.astype(BF16), jnp.uint16).astype(jnp.uint32)
    words = (wb[:, d // 2:] << 16) | wb[:, :d // 2]
    return lax.bitcast_convert_type(words, I32).reshape(n * WORDS_PER_ROW, LANES)


def _split_bf16(x):
    hi = x.astype(BF16)
    lo = (x - hi.astype(F32)).astype(BF16)
    return hi, lo


def _feature_row_mask(width):
    r = lax.broadcasted_iota(I32, (SUBLANES, width), 0)
    k = lax.broadcasted_iota(I32, (SUBLANES, width), 1)
    return (k & 7) == 2 * (r & 3) + (r >> 2)


def _expand_matrix():
    e = np.arange(N_SEL)[:, None]
    k = np.arange(SUBLANES * N_SEL)[None, :]
    return jnp.asarray((k // SUBLANES == e), BF16)


def _gather_group(idx_ref, t, tab_ref, tiles_ref, experts):
    ids = [idx_ref.at[t + k] for k in range(PEER_GROUP)]
    for e in experts:
        for k in range(PEER_GROUP):
            row = pl.multiple_of(ids[k][e], WORDS_PER_ROW)
            tiles_ref[k, pl.ds(e * WORDS_PER_ROW, WORDS_PER_ROW), :] = tab_ref[pl.ds(row, WORDS_PER_ROW), :]


def _token_pipeline(tb, idx_ref, tab_ref, stages, tiles_a, tiles_b):
    g = PEER_GROUP
    _gather_group(idx_ref, 0, tab_ref, tiles_a, range(N_SEL))

    def phase(t0, src, dst):
        tn = jnp.minimum(t0 + g, tb - g)
        pieces = list(stages(t0, src))
        per = N_SEL // PEER_STAGES
        for s in range(PEER_STAGES):
            pieces[s]()
            _gather_group(idx_ref, tn, tab_ref, dst, range(s * per, (s + 1) * per))
        for piece in pieces[PEER_STAGES:]:
            piece()

    def body(i, carry):
        t0 = pl.multiple_of(i * g, g)
        pl.when((i & 1) == 0)(lambda: phase(t0, tiles_a, tiles_b))
        pl.when((i & 1) == 1)(lambda: phase(t0, tiles_b, tiles_a))
        return carry

    lax.fori_loop(0, tb // g, body, 0)


def _tile_chunk(tiles_ref, k, c):
    rows = WORDS_PER_ROW * N_SEL // PEER_STAGES
    return pltpu.bitcast(tiles_ref[k, pl.ds(c * rows, rows), :], BF16)


def _peer_u_kernel(idx_ref, m_ref, gate_ref, tab_ref, ex_ref, ext_ref, w8_ref, q_ref, tiles_a, tiles_b):
    tb = m_ref.shape[0]
    cw = SUBLANES * N_SEL // PEER_STAGES
    mask = _feature_row_mask(cw)

    def stages(t0, src):
        x16 = []

        def prep_and_chunk0():
            for k in range(PEER_GROUP):
                xh, xl = _split_bf16(m_ref[t0 + k])
                x16.append(jnp.concatenate([xh, xl], axis=0))
            chunk(0)

        def chunk(c):
            for k in range(PEER_GROUP):
                p = _dot_nt(x16[k], _tile_chunk(src, k, c))
                row = pl.multiple_of((t0 + k) * SUBLANES, SUBLANES)
                q_ref[pl.ds(row, SUBLANES), c * cw:(c + 1) * cw] = jnp.where(mask, p[:SUBLANES] + p[SUBLANES:], 0.0)

        return [prep_and_chunk0] + [functools.partial(chunk, c) for c in range(1, PEER_STAGES)]

    _token_pipeline(tb, idx_ref, tab_ref, stages, tiles_a, tiles_b)

    qh, ql = _split_bf16(q_ref[...])
    r = _dot(qh, ext_ref[...]) + _dot(ql, ext_ref[...])
    a = jnp.sum(r.reshape(tb, SUBLANES, N_SEL), axis=1)
    gelu = 0.5 * a * (1.0 + lax.erf(a * (2.0 ** -0.5)))
    wh, wl = _split_bf16(gelu * gate_ref[...])
    w8_ref[...] = _dot(wh, ex_ref[...]) + _dot(wl, ex_ref[...])


def _peer_u(idx, m3, gates, table, ex, ext):
    t = idx.shape[0]
    tb = PEER_TILE
    wide = SUBLANES * N_SEL
    tiles = pltpu.VMEM((PEER_GROUP, WORDS_PER_ROW * N_SEL, LANES), I32)
    return pl.pallas_call(
        _peer_u_kernel,
        grid=(t // tb,),
        in_specs=[
            pl.BlockSpec((tb, N_SEL), lambda i: (i, 0), memory_space=pltpu.SMEM),
            pl.BlockSpec((tb, SUBLANES, LANES), lambda i: (i, 0, 0)),
            pl.BlockSpec((tb, N_SEL), lambda i: (i, 0)),
            pl.BlockSpec(table.shape, lambda i: (0, 0)),
            pl.BlockSpec((N_SEL, wide), lambda i: (0, 0)),
            pl.BlockSpec((wide, N_SEL), lambda i: (0, 0)),
        ],
        out_specs=pl.BlockSpec((tb, wide), lambda i: (i, 0)),
        out_shape=jax.ShapeDtypeStruct((t, wide), F32),
        scratch_shapes=[pltpu.VMEM((tb * SUBLANES, wide), F32), tiles, tiles],
        compiler_params=_cparams(("arbitrary",), VMEM_LIMIT_TABLE),
        name="peer_u",
    )(idx, m3, gates, table, ex, ext)


def _peer_v_kernel(idx_ref, w8_ref, h_ref, tab_ref, o_ref, tiles_a, tiles_b):
    tb = h_ref.shape[0]
    wide = SUBLANES * N_SEL
    cw = wide // PEER_STAGES
    mask = _feature_row_mask(wide)

    def stages(t0, src):
        w16 = []
        parts = [[] for _ in range(PEER_GROUP)]

        def prep_and_chunk0():
            for k in range(PEER_GROUP):
                wrow = jnp.broadcast_to(w8_ref[pl.ds(t0 + k, 1), :], mask.shape)
                wh, wl = _split_bf16(jnp.where(mask, wrow, 0.0))
                w16.append(jnp.concatenate([wh, wl], axis=0))
            chunk(0)

        def chunk(c):
            for k in range(PEER_GROUP):
                parts[k].append(_dot(w16[k][:, c * cw:(c + 1) * cw], _tile_chunk(src, k, c)))

        def finish():
            for k in range(PEER_GROUP):
                o16 = (parts[k][0] + parts[k][1]) + (parts[k][2] + parts[k][3])
                o_ref[t0 + k] = h_ref[t0 + k] + (o16[:SUBLANES] + o16[SUBLANES:])

        return [prep_and_chunk0] + [functools.partial(chunk, c) for c in range(1, PEER_STAGES)] + [finish]

    _token_pipeline(tb, idx_ref, tab_ref, stages, tiles_a, tiles_b)


def _peer_v(idx, w8, h3, table):
    t = idx.shape[0]
    tb = PEER_TILE
    wide = SUBLANES * N_SEL
    tiles = pltpu.VMEM((PEER_GROUP, WORDS_PER_ROW * N_SEL, LANES), I32)
    return pl.pallas_call(
        _peer_v_kernel,
        grid=(t // tb,),
        in_specs=[
            pl.BlockSpec((tb, N_SEL), lambda i: (i, 0), memory_space=pltpu.SMEM),
            pl.BlockSpec((tb, wide), lambda i: (i, 0)),
            pl.BlockSpec((tb, SUBLANES, LANES), lambda i: (i, 0, 0)),
            pl.BlockSpec(table.shape, lambda i: (0, 0)),
        ],
        out_specs=pl.BlockSpec((tb, SUBLANES, LANES), lambda i: (i, 0, 0)),
        out_shape=jax.ShapeDtypeStruct(h3.shape, F32),
        scratch_shapes=[tiles, tiles],
        compiler_params=_cparams(("arbitrary",), VMEM_LIMIT_TABLE),
        name="peer_v",
    )(idx, w8, h3, table)


def _ple_kernel(h_ref, p_ref, gn_ref, wg_ref, bg_ref, wp_ref, fn_ref, o_ref, *, final):
    h = h_ref[...]
    n = (_rms(h, 1e-6) * gn_ref[...]).astype(BF16)
    z = _dot(n, wg_ref[...]) + bg_ref[...]
    gate = 1.0 / (1.0 + jnp.exp(-z))
    h = h + gate * _dot(p_ref[...].astype(BF16), wp_ref[...])
    if final:
        h = _rms(h, 1e-6) * fn_ref[...]
    o_ref[...] = h


def _ple(h2, p2, gn, wg, bg, wp, fn, final):
    t, d = h2.shape
    pd = p2.shape[1]
    row = lambda i: (i, 0)
    full = lambda i: (0, 0)
    return pl.pallas_call(
        functools.partial(_ple_kernel, final=final),
        grid=(t // ROW_TILE,),
        in_specs=[
            pl.BlockSpec((ROW_TILE, d), row),
            pl.BlockSpec((ROW_TILE, pd), row),
            pl.BlockSpec((1, d), full),
            pl.BlockSpec((d, d), full),
            pl.BlockSpec((1, d), full),
            pl.BlockSpec((pd, d), full),
            pl.BlockSpec((1, d), full),
        ],
        out_specs=pl.BlockSpec((ROW_TILE, d), row),
        out_shape=jax.ShapeDtypeStruct((t, d), F32),
        compiler_params=_cparams(("parallel",)),
        name="ple_final",
    )(h2, p2, gn, wg, bg, wp, fn)


def _permute_w_in(w):
    d = w.shape[0]
    hq, dq = DIFF_HEADS, DIFF_QK_DIM
    n_qk = 2 * hq * dq
    n_v = hq * DIFF_V_DIM
    swap_qk = lambda c: c.reshape(d, 2, hq, dq).transpose(0, 2, 1, 3).reshape(d, n_qk)
    wa = jnp.concatenate([swap_qk(w[:, :n_qk]), swap_qk(w[:, n_qk:2 * n_qk]), w[:, 2 * n_qk:2 * n_qk + n_v]],
                         axis=1)
    base = 2 * n_qk + n_v
    hr, dr = RET_HEADS, RET_QK_DIM
    n_r = hr * dr
    swap_r = lambda c: c.reshape(d, hr, 2, dr // 2).transpose(0, 2, 1, 3).reshape(d, n_r)
    wr = jnp.concatenate([swap_r(w[:, base:base + n_r]), swap_r(w[:, base + n_r:base + 2 * n_r]),
                          w[:, base + 2 * n_r:]], axis=1)
    return wa.astype(BF16), wr.astype(BF16)


def kernel(x, p, attn_norm, w_in, lam_q1, lam_k1, lam_q2, lam_k2, subln_gain, w_out, rel_bias, ffn_norm,
           peer_query, peer_subkeys, peer_u, peer_v, ple_norm, ple_gate_w, ple_gate_b, ple_proj, final_norm):
    b, s, d = x.shape
    t = b * s
    depth = w_in.shape[0]
    bias = _bias_tiles(rel_bias, ATT_TILE)
    n_diff = DIFF_HEADS * DIFF_V_DIM
    ex = _expand_matrix()
    h = x.reshape(t, d)
    for layer in range(depth):
        lambda_init = 0.8 - 0.6 * math.exp(-0.3 * layer)
        wa, wr = _permute_w_in(w_in[layer])
        pa, pr = _in_proj(h, attn_norm[layer][None, :], wa, wr)

        lamv = jnp.stack([lam_q1[layer], lam_k1[layer], lam_q2[layer], lam_k2[layer]]).astype(F32)
        diff_out = _diff_attn(pa.reshape(b, s, -1), bias, lamv, subln_gain[layer][None, :].astype(F32),
                              lambda_init)
        ret_out = _retention(pr.reshape(b, s, -1), s)

        wo = w_out[layer].astype(BF16)
        h, m, qry = _out_proj(h, diff_out.reshape(t, -1), ret_out.reshape(t, -1), wo[:n_diff], wo[n_diff:],
                              ffn_norm[layer][None, :], peer_query[layer].astype(BF16))

        subkeys = peer_subkeys[layer].reshape(2 * PEER_HEADS, PEER_N_KEYS, PEER_KEY_DIM).astype(BF16)
        idx_t, gate_t = _peer_topk(qry, subkeys)
        idx = idx_t.T
        gates = gate_t.T
        w8 = _peer_u(idx, m.reshape(t, SUBLANES, LANES), gates, _pack_table(peer_u[layer]), ex, ex.T)
        h = _peer_v(idx, w8, h.reshape(t, SUBLANES, LANES), _pack_table(peer_v[layer])).reshape(t, d)

        h = _ple(h, p[layer].reshape(t, -1), ple_norm[layer][None, :], ple_gate_w[layer].astype(BF16),
                 ple_gate_b[layer][None, :], ple_proj[layer].astype(BF16), final_norm[None, :],
                 final=(layer == depth - 1))
    return h.reshape(b, s, d)
```

```python
import functools
import math

import numpy as np
import jax
import jax.numpy as jnp
from jax import lax
from jax.experimental import pallas as pl
from jax.experimental.pallas import tpu as pltpu

F32 = jnp.float32
BF16 = jnp.bfloat16
I32 = jnp.int32

DIFF_HEADS = 4
DIFF_V_DIM = 128
DIFF_QK_DIM = 64
RET_HEADS = 4
RET_V_DIM = 128
RET_QK_DIM = 64
RET_CHUNK = 128
ROPE_BASE = 10000.0
T5_BUCKETS = 32
T5_MAX_EXACT = 16
T5_MAX_DISTANCE = 128
PEER_HEADS = 8
PEER_N_KEYS = 128
PEER_TOPK = 16
PEER_KEY_DIM = 128
N_SEL = PEER_HEADS * PEER_TOPK

LANES = 128
SUBLANES = 8
VMEM_LIMIT = 48 * 1024 * 1024
VMEM_LIMIT_TABLE = 56 * 1024 * 1024

ROW_TILE = 256
ATT_TILE = 512
TOPK_TILE = 256
PEER_TILE = 128
PEER_GROUP = 4
PEER_STAGES = 4
WORDS_PER_ROW = 4
NEG_BIG = -1e30


def _cparams(sem, limit=VMEM_LIMIT):
    return pltpu.CompilerParams(dimension_semantics=sem, vmem_limit_bytes=limit)


def _dot(a, b):
    return jnp.dot(a, b, preferred_element_type=F32)


def _dot_nt(a, b):
    return lax.dot_general(a, b, (((1,), (1,)), ((), ())), preferred_element_type=F32)


def _rms(x, eps):
    return x * lax.rsqrt(jnp.mean(x * x, axis=-1, keepdims=True) + eps)


def _in_proj_kernel(x_ref, g_ref, wa_ref, wr_ref, oa_ref, or_ref):
    a = (_rms(x_ref[...], 1e-6) * g_ref[...]).astype(BF16)
    oa_ref[...] = _dot(a, wa_ref[...]).astype(BF16)
    or_ref[...] = _dot(a, wr_ref[...])


def _in_proj(x2, gain, wa, wr):
    t, d = x2.shape
    na, nr = wa.shape[1], wr.shape[1]
    return pl.pallas_call(
        _in_proj_kernel,
        grid=(t // ROW_TILE,),
        in_specs=[
            pl.BlockSpec((ROW_TILE, d), lambda i: (i, 0)),
            pl.BlockSpec((1, d), lambda i: (0, 0)),
            pl.BlockSpec((d, na), lambda i: (0, 0)),
            pl.BlockSpec((d, nr), lambda i: (0, 0)),
        ],
        out_specs=[
            pl.BlockSpec((ROW_TILE, na), lambda i: (i, 0)),
            pl.BlockSpec((ROW_TILE, nr), lambda i: (i, 0)),
        ],
        out_shape=[jax.ShapeDtypeStruct((t, na), BF16), jax.ShapeDtypeStruct((t, nr), F32)],
        compiler_params=_cparams(("parallel",)),
        name="in_proj",
    )(x2, gain, wa, wr)


def _diff_attn_kernel(q_ref, k_ref, v_ref, bias_ref, lam_ref, g_ref, o_ref, vt_ref, acc_ref, *, lambda_init):
    i = pl.program_id(2)
    tq = q_ref.shape[0]
    tk = tq
    s_len = k_ref.shape[0]

    @pl.when(i == 0)
    def _():
        for c in range(s_len // tk):
            vt_ref[:, c * tk:(c + 1) * tk] = v_ref[c * tk:(c + 1) * tk, :].astype(F32).T.astype(BF16)

    q = q_ref[...] * jnp.asarray(DIFF_QK_DIM ** -0.5, BF16)
    lane = lax.broadcasted_iota(I32, q.shape, 1)
    zero = jnp.zeros_like(q)
    qm = (jnp.where(lane < DIFF_QK_DIM, q, zero), jnp.where(lane >= DIFF_QK_DIM, q, zero))
    acc_ref[...] = jnp.zeros_like(acc_ref)

    def body(j, carry):
        off = pl.multiple_of(j * tk, tk)
        k = k_ref[pl.ds(off, tk), :]
        vt = vt_ref[:, pl.ds(off, tk)]
        bias = bias_ref[jnp.minimum(i - j, 2)]
        scores = [_dot_nt(k, qm[mp]) + bias for mp in range(2)]
        out = []
        for mp in range(2):
            m, l = carry[2 * mp], carry[2 * mp + 1]
            s = scores[mp]
            mn = jnp.maximum(m, jnp.max(s, axis=0, keepdims=True))
            p = jnp.exp(s - mn)
            alpha = jnp.exp(m - mn)
            l = alpha * l + jnp.sum(p, axis=0, keepdims=True)
            acc_ref[mp] = alpha * acc_ref[mp] + _dot(vt, p.astype(BF16))
            out += [mn, l]
        return tuple(out)

    mi = jnp.full((1, tq), NEG_BIG, F32)
    li = jnp.zeros((1, tq), F32)
    m1, l1, m2, l2 = lax.fori_loop(0, i + 1, body, (mi, li, mi, li))

    lv = lam_ref[...]
    lam = (jnp.exp(jnp.sum(lv[0:1] * lv[1:2], axis=-1, keepdims=True))
           - jnp.exp(jnp.sum(lv[2:3] * lv[3:4], axis=-1, keepdims=True)) + lambda_init)
    o = acc_ref[0] / l1 - lam * (acc_ref[1] / l2)
    o = o * lax.rsqrt(jnp.mean(o * o, axis=0, keepdims=True) + 1e-5)
    o_ref[...] = (o.T * g_ref[...] * (1.0 - lambda_init)).astype(o_ref.dtype)


def _t5_bucket_starts():
    n = np.arange(2 * T5_MAX_DISTANCE, dtype=np.int32)
    nf = np.maximum(n, 1).astype(np.float32)
    large = T5_MAX_EXACT + (np.log(nf / np.float32(T5_MAX_EXACT))
                            / np.float32(math.log(T5_MAX_DISTANCE / T5_MAX_EXACT))
                            * np.float32(T5_BUCKETS - T5_MAX_EXACT)).astype(np.int32)
    bucket = np.where(n < T5_MAX_EXACT, n, np.minimum(large, T5_BUCKETS - 1))
    assert np.all(np.diff(bucket) >= 0) and bucket[-1] == T5_BUCKETS - 1
    return [int(np.argmax(bucket >= b)) for b in range(T5_BUCKETS)]


def _bias_tiles(rel_bias, t):
    assert t >= T5_MAX_DISTANCE

    first = _t5_bucket_starts()
    rb = rel_bias.astype(F32)

    def lookup(dist):
        out = jnp.broadcast_to(rb[T5_BUCKETS - 1][:, None, None], (rb.shape[1],) + dist.shape)
        for b in range(T5_BUCKETS - 2, -1, -1):
            out = jnp.where((dist < first[b + 1])[None], rb[b][:, None, None], out)
        return out

    r = jnp.arange(t, dtype=I32)[None, :]
    c = jnp.arange(t, dtype=I32)[:, None]
    d0 = r - c
    diag = jnp.where((d0 >= 0)[None], lookup(jnp.maximum(d0, 0)), NEG_BIG)
    near = lookup(d0 + t)
    far = lookup(jnp.full((t, t), 2 * t, I32))
    return jnp.stack([diag, near, far], axis=1)


def _diff_attn(qkv, bias, lamv, gain, lambda_init):
    b, s, _ = qkv.shape
    h = DIFF_HEADS
    t = ATT_TILE
    kern = functools.partial(_diff_attn_kernel, lambda_init=lambda_init)
    return pl.pallas_call(
        kern,
        grid=(b, h, s // t),
        in_specs=[
            pl.BlockSpec((None, t, LANES), lambda bi, hi, i: (bi, i, hi)),
            pl.BlockSpec((None, s, LANES), lambda bi, hi, i: (bi, 0, h + hi)),
            pl.BlockSpec((None, s, LANES), lambda bi, hi, i: (bi, 0, 2 * h + hi)),
            pl.BlockSpec((None, 3, t, t), lambda bi, hi, i: (hi, 0, 0, 0)),
            pl.BlockSpec((4, DIFF_QK_DIM), lambda bi, hi, i: (0, 0)),
            pl.BlockSpec((1, DIFF_V_DIM), lambda bi, hi, i: (0, 0)),
        ],
        out_specs=pl.BlockSpec((None, t, LANES), lambda bi, hi, i: (bi, i, hi)),
        out_shape=jax.ShapeDtypeStruct((b, s, h * DIFF_V_DIM), BF16),
        scratch_shapes=[pltpu.VMEM((DIFF_V_DIM, s), BF16), pltpu.VMEM((2, DIFF_V_DIM, t), F32)],
        compiler_params=_cparams(("parallel", "parallel", "arbitrary")),
        name="diff_attn",
    )(qkv, qkv, qkv, bias, lamv, gain)


def _retention_kernel(q_ref, k_ref, v_ref, g_ref, cos_ref, sin_ref, decay_ref, zeta_ref, xi_ref,
                      cd_ref, o_ref, state_ref):
    @pl.when(pl.program_id(1) == 0)
    def _():
        state_ref[...] = jnp.zeros_like(state_ref)

    cos = cos_ref[...]
    sin = sin_ref[...]

    def rotate(x):
        x1 = x[:, :LANES]
        x2 = x[:, LANES:]
        return jnp.concatenate([x1 * cos - x2 * sin, x1 * sin + x2 * cos], axis=1)

    qr = rotate(q_ref[...])
    kr = rotate(k_ref[...]) * (RET_QK_DIM ** -0.5)
    qx = qr * xi_ref[...]
    kz = kr * zeta_ref[...]
    lane = lax.broadcasted_iota(I32, qr.shape, 1)
    head_of_lane = (lane & (LANES - 1)) >> 5
    zero = jnp.zeros_like(qr)
    for h in range(RET_HEADS):
        msk = head_of_lane == h
        qh = jnp.where(msk, qr, zero).astype(BF16)
        kh = jnp.where(msk, kr, zero).astype(BF16)
        vh = v_ref[:, h * RET_V_DIM:(h + 1) * RET_V_DIM].astype(BF16)
        scores = _dot_nt(qh, kh) * decay_ref[h]
        inner = _dot(scores.astype(BF16), vh)
        st = state_ref[h]
        cross = _dot(jnp.where(msk, qx, zero).astype(BF16), st.astype(BF16))
        kzh = jnp.where(msk, kz, zero).astype(BF16)
        kv = lax.dot_general(kzh, vh, (((0,), (0,)), ((), ())), preferred_element_type=F32)
        state_ref[h] = cd_ref[h] * st + kv
        r = _rms(inner + cross, 1e-6)
        g = g_ref[:, h * RET_V_DIM:(h + 1) * RET_V_DIM]
        silu = g * (1.0 / (1.0 + jnp.exp(-g)))
        o_ref[:, h * RET_V_DIM:(h + 1) * RET_V_DIM] = (r * silu).astype(o_ref.dtype)


def _retention(pr, s):
    b = pr.shape[0]
    c = RET_CHUNK
    hh = RET_HEADS
    half = RET_QK_DIM // 2
    gamma = 1.0 - jnp.exp2(-5.0 - jnp.arange(hh, dtype=F32))
    lg = jnp.log(gamma)
    i = jnp.arange(c, dtype=F32)
    rel = i[:, None] - i[None, :]
    decay = jnp.where(rel >= 0, jnp.exp(jnp.maximum(rel, 0.0)[None] * lg[:, None, None]), 0.0)
    zeta = jnp.exp((c - 1 - i)[None, :] * lg[:, None])
    xi = jnp.exp((i + 1)[None, :] * lg[:, None])
    cd = jnp.exp(c * lg)
    head_of_lane = (np.arange(2 * LANES) % LANES) // half
    zeta_tab = zeta.T[:, head_of_lane]
    xi_tab = xi.T[:, head_of_lane]
    pos = jnp.arange(s, dtype=F32)
    freqs = 1.0 / (ROPE_BASE ** (jnp.arange(half, dtype=F32) / half))
    ang = pos[:, None] * freqs[None, :]
    cos_tab = jnp.tile(jnp.cos(ang), (1, hh))
    sin_tab = jnp.tile(jnp.sin(ang), (1, hh))
    w = hh * RET_V_DIM
    return pl.pallas_call(
        _retention_kernel,
        grid=(b, s // c),
        in_specs=[
            pl.BlockSpec((None, c, 2 * LANES), lambda bi, n: (bi, n, 0)),
            pl.BlockSpec((None, c, 2 * LANES), lambda bi, n: (bi, n, 1)),
            pl.BlockSpec((None, c, w), lambda bi, n: (bi, n, 1)),
            pl.BlockSpec((None, c, w), lambda bi, n: (bi, n, 2)),
            pl.BlockSpec((c, LANES), lambda bi, n: (n, 0)),
            pl.BlockSpec((c, LANES), lambda bi, n: (n, 0)),
            pl.BlockSpec((hh, c, c), lambda bi, n: (0, 0, 0)),
            pl.BlockSpec((c, 2 * LANES), lambda bi, n: (0, 0)),
            pl.BlockSpec((c, 2 * LANES), lambda bi, n: (0, 0)),
            pl.BlockSpec(memory_space=pltpu.SMEM),
        ],
        out_specs=pl.BlockSpec((None, c, w), lambda bi, n: (bi, n, 0)),
        out_shape=jax.ShapeDtypeStruct((b, s, w), BF16),
        scratch_shapes=[pltpu.VMEM((hh, 2 * LANES, RET_V_DIM), F32)],
        compiler_params=_cparams(("parallel", "arbitrary")),
        name="retention",
    )(pr, pr, pr, pr, cos_tab, sin_tab, decay, zeta_tab, xi_tab, cd)


def _out_proj_kernel(x_ref, d_ref, r_ref, wa_ref, wb_ref, g_ref, wq_ref, h_ref, m_ref, q_ref):
    h = x_ref[...] + _dot(d_ref[...], wa_ref[...]) + _dot(r_ref[...], wb_ref[...])
    h_ref[...] = h
    m = _rms(h, 1e-6) * g_ref[...]
    m_ref[...] = m
    q_ref[...] = _dot(m.astype(BF16), wq_ref[...]).astype(BF16)


def _out_proj(x2, d2, r2, wa, wb, gain, wq):
    t, d = x2.shape
    nq = wq.shape[1]
    kd, kr = d2.shape[1], r2.shape[1]
    row = lambda i: (i, 0)
    full = lambda i: (0, 0)
    return pl.pallas_call(
        _out_proj_kernel,
        grid=(t // ROW_TILE,),
        in_specs=[
            pl.BlockSpec((ROW_TILE, d), row),
            pl.BlockSpec((ROW_TILE, kd), row),
            pl.BlockSpec((ROW_TILE, kr), row),
            pl.BlockSpec((kd, d), full),
            pl.BlockSpec((kr, d), full),
            pl.BlockSpec((1, d), full),
            pl.BlockSpec((d, nq), full),
        ],
        out_specs=[
            pl.BlockSpec((ROW_TILE, d), row),
            pl.BlockSpec((ROW_TILE, d), row),
            pl.BlockSpec((ROW_TILE, nq), row),
        ],
        out_shape=[jax.ShapeDtypeStruct((t, d), F32), jax.ShapeDtypeStruct((t, d), F32),
                   jax.ShapeDtypeStruct((t, nq), BF16)],
        compiler_params=_cparams(("parallel",)),
        name="out_proj",
    )(x2, d2, r2, wa, wb, gain, wq)


def _topk_rows(problems, k):
    neg_inf = jnp.asarray(-jnp.inf, F32)
    big = jnp.asarray(jnp.inf, F32)
    state = [s for s, _ in problems]
    vals = [[] for _ in problems]
    picked = [[] for _ in problems]
    for _ in range(k):
        mx = [jnp.max(s, axis=0, keepdims=True) for s in state]
        key = [jnp.min(jnp.where(s == m, keys, big), axis=0, keepdims=True)
               for s, m, (_, keys) in zip(state, mx, problems)]
        state = [jnp.where(keys == ky, neg_inf, s) for s, ky, (_, keys) in zip(state, key, problems)]
        for n in range(len(problems)):
            vals[n].append(mx[n])
            picked[n].append(key[n])
    return vals, picked


def _staircase(v1, v2, i1, i2):
    k = PEER_TOPK
    tb = v1.shape[1]
    row = lax.broadcasted_iota(I32, (SUBLANES, tb), 0)
    neg_inf = jnp.asarray(-jnp.inf, F32)
    vals, keys = [], []

    def add(val, pos, expert, valid=None):
        key = (pos * (PEER_N_KEYS * PEER_N_KEYS) + expert).astype(F32)
        vals.append(val if valid is None else jnp.where(valid, val, neg_inf))
        keys.append(key)

    for a in range(SUBLANES):
        bmax = k // (a + 1) - 1
        for b0 in range(0, bmax + 1, SUBLANES):
            b = b0 + row
            add(v1[a:a + 1] + v2[b0:b0 + SUBLANES], a * k + b, i1[a:a + 1] * PEER_N_KEYS + i2[b0:b0 + SUBLANES],
                None if b0 + SUBLANES - 1 <= bmax else b <= bmax)
    add(v1[SUBLANES:] + v2[0:1], (SUBLANES + row) * k, i1[SUBLANES:] * PEER_N_KEYS + i2[0:1])
    return jnp.concatenate(vals, axis=0), jnp.concatenate(keys, axis=0)


def _peer_topk_kernel(q_ref, sk_ref, idx_ref, gate_ref):
    k = PEER_TOPK
    tb = q_ref.shape[0]
    key_ids = lax.broadcasted_iota(I32, (PEER_N_KEYS, tb), 0).astype(F32)
    halves = [(_dot_nt(sk_ref[c], q_ref[:, c * PEER_KEY_DIM:(c + 1) * PEER_KEY_DIM]), key_ids)
              for c in range(2)]
    vals, ids = _topk_rows(halves, k)
    half_vals = [jnp.concatenate(v, axis=0) for v in vals]
    half_ids = [jnp.concatenate(i, axis=0).astype(I32) for i in ids]
    cand, cand_keys = _staircase(half_vals[0], half_vals[1], half_ids[0], half_ids[1])
    (top_s,), (top_key,) = _topk_rows([(cand, cand_keys)], k)
    ts = jnp.concatenate(top_s, axis=0)
    ex = jnp.exp(ts - ts[0:1])
    gate_ref[...] = ex / jnp.sum(ex, axis=0, keepdims=True)
    expert = jnp.concatenate(top_key, axis=0).astype(I32) & (PEER_N_KEYS * PEER_N_KEYS - 1)
    idx_ref[...] = expert * WORDS_PER_ROW


def _peer_topk(qry, subkeys):
    t = qry.shape[0]
    tb = TOPK_TILE
    k = PEER_TOPK
    return pl.pallas_call(
        _peer_topk_kernel,
        grid=(t // tb, PEER_HEADS),
        in_specs=[
            pl.BlockSpec((tb, 2 * PEER_KEY_DIM), lambda i, h: (i, h)),
            pl.BlockSpec((2, PEER_N_KEYS, PEER_KEY_DIM), lambda i, h: (h, 0, 0)),
        ],
        out_specs=[
            pl.BlockSpec((k, tb), lambda i, h: (h, i)),
            pl.BlockSpec((k, tb), lambda i, h: (h, i)),
        ],
        out_shape=[jax.ShapeDtypeStruct((N_SEL, t), I32), jax.ShapeDtypeStruct((N_SEL, t), F32)],
        compiler_params=_cparams(("parallel", "parallel")),
        name="peer_topk",
    )(qry, subkeys)


def _pack_table(w):
    n, d = w.shape
    wb = w.astype(BF16)
    pairs = jnp.stack([wb[:, :d // 2], wb[:, d // 2:]], axis=-1)
    return lax.bitcast_convert_type(pairs, I32).reshape(n * WORDS_PER_ROW, LANES)


def _split_bf16(x):
    hi = x.astype(BF16)
    lo = (x - hi.astype(F32)).astype(BF16)
    return hi, lo


def _feature_row_mask(width):
    r = lax.broadcasted_iota(I32, (SUBLANES, width), 0)
    k = lax.broadcasted_iota(I32, (SUBLANES, width), 1)
    return (k & 7) == 2 * (r & 3) + (r >> 2)


def _expand_matrix():
    e = np.arange(N_SEL)[:, None]
    k = np.arange(SUBLANES * N_SEL)[None, :]
    return jnp.asarray((k // SUBLANES == e), BF16)


def _gather_group(idx_ref, t, tab_ref, tiles_ref, experts):
    ids = [idx_ref.at[t + k] for k in range(PEER_GROUP)]
    for e in experts:
        for k in range(PEER_GROUP):
            row = pl.multiple_of(ids[k][e], WORDS_PER_ROW)
            tiles_ref[k, pl.ds(e * WORDS_PER_ROW, WORDS_PER_ROW), :] = tab_ref[pl.ds(row, WORDS_PER_ROW), :]


def _token_pipeline(tb, idx_ref, tab_ref, stages, tiles_a, tiles_b):
    g = PEER_GROUP
    _gather_group(idx_ref, 0, tab_ref, tiles_a, range(N_SEL))

    def phase(t0, src, dst):
        tn = jnp.minimum(t0 + g, tb - g)
        pieces = list(stages(t0, src))
        per = N_SEL // PEER_STAGES
        for s in range(PEER_STAGES):
            pieces[s]()
            _gather_group(idx_ref, tn, tab_ref, dst, range(s * per, (s + 1) * per))
        for piece in pieces[PEER_STAGES:]:
            piece()

    def body(i, carry):
        t0 = pl.multiple_of(i * g, g)
        pl.when((i & 1) == 0)(lambda: phase(t0, tiles_a, tiles_b))
        pl.when((i & 1) == 1)(lambda: phase(t0, tiles_b, tiles_a))
        return carry

    lax.fori_loop(0, tb // g, body, 0)


def _tile_chunk(tiles_ref, k, c):
    rows = WORDS_PER_ROW * N_SEL // PEER_STAGES
    return pltpu.bitcast(tiles_ref[k, pl.ds(c * rows, rows), :], BF16)


def _peer_u_kernel(idx_ref, m_ref, gate_ref, tab_ref, ex_ref, ext_ref, w8_ref, q_ref, tiles_a, tiles_b):
    tb = m_ref.shape[0]
    cw = SUBLANES * N_SEL // PEER_STAGES
    mask = _feature_row_mask(cw)

    def stages(t0, src):
        x16 = []

        def prep_and_chunk0():
            for k in range(PEER_GROUP):
                x = m_ref[pl.ds(t0 + k, 1), :].reshape(SUBLANES, LANES)
                xh, xl = _split_bf16(x)
                x16.append(jnp.concatenate([xh, xl], axis=0))
            chunk(0)

        def chunk(c):
            for k in range(PEER_GROUP):
                p = _dot_nt(x16[k], _tile_chunk(src, k, c))
                picked = jnp.where(mask, p[:SUBLANES] + p[SUBLANES:], 0.0)
                q_ref[pl.ds(t0 + k, 1), c * cw:(c + 1) * cw] = jnp.sum(picked, axis=0, keepdims=True)

        return [prep_and_chunk0] + [functools.partial(chunk, c) for c in range(1, PEER_STAGES)]

    _token_pipeline(tb, idx_ref, tab_ref, stages, tiles_a, tiles_b)

    qh, ql = _split_bf16(q_ref[...])
    a = _dot(qh, ext_ref[...]) + _dot(ql, ext_ref[...])
    gelu = 0.5 * a * (1.0 + lax.erf(a * (2.0 ** -0.5)))
    wh, wl = _split_bf16(gelu * gate_ref[...])
    w8_ref[...] = _dot(wh, ex_ref[...]) + _dot(wl, ex_ref[...])


def _peer_u(idx, m2, gates, table, ex, ext):
    t = idx.shape[0]
    tb = PEER_TILE
    wide = SUBLANES * N_SEL
    tiles = pltpu.VMEM((PEER_GROUP, WORDS_PER_ROW * N_SEL, LANES), I32)
    return pl.pallas_call(
        _peer_u_kernel,
        grid=(t // tb,),
        in_specs=[
            pl.BlockSpec((tb, N_SEL), lambda i: (i, 0), memory_space=pltpu.SMEM),
            pl.BlockSpec((tb, wide), lambda i: (i, 0)),
            pl.BlockSpec((tb, N_SEL), lambda i: (i, 0)),
            pl.BlockSpec(table.shape, lambda i: (0, 0)),
            pl.BlockSpec((N_SEL, wide), lambda i: (0, 0)),
            pl.BlockSpec((wide, N_SEL), lambda i: (0, 0)),
        ],
        out_specs=pl.BlockSpec((tb, wide), lambda i: (i, 0)),
        out_shape=jax.ShapeDtypeStruct((t, wide), F32),
        scratch_shapes=[pltpu.VMEM((tb, wide), F32), tiles, tiles],
        compiler_params=_cparams(("arbitrary",), VMEM_LIMIT_TABLE),
        name="peer_u",
    )(idx, m2, gates, table, ex, ext)


def _peer_v_kernel(idx_ref, w8_ref, h_ref, tab_ref, o_ref, tiles_a, tiles_b):
    tb = h_ref.shape[0]
    wide = SUBLANES * N_SEL
    cw = wide // PEER_STAGES
    mask = _feature_row_mask(wide)

    def stages(t0, src):
        w16 = []
        parts = [[] for _ in range(PEER_GROUP)]

        def prep_and_chunk0():
            for k in range(PEER_GROUP):
                wrow = jnp.broadcast_to(w8_ref[pl.ds(t0 + k, 1), :], mask.shape)
                wh, wl = _split_bf16(jnp.where(mask, wrow, 0.0))
                w16.append(jnp.concatenate([wh, wl], axis=0))
            chunk(0)

        def chunk(c):
            for k in range(PEER_GROUP):
                parts[k].append(_dot(w16[k][:, c * cw:(c + 1) * cw], _tile_chunk(src, k, c)))

        def finish():
            for k in range(PEER_GROUP):
                o16 = (parts[k][0] + parts[k][1]) + (parts[k][2] + parts[k][3])
                out = (o16[:SUBLANES] + o16[SUBLANES:]).reshape(1, wide)
                o_ref[pl.ds(t0 + k, 1), :] = h_ref[pl.ds(t0 + k, 1), :] + out

        return [prep_and_chunk0] + [functools.partial(chunk, c) for c in range(1, PEER_STAGES)] + [finish]

    _token_pipeline(tb, idx_ref, tab_ref, stages, tiles_a, tiles_b)


def _peer_v(idx, w8, h2, table):
    t = idx.shape[0]
    tb = PEER_TILE
    wide = SUBLANES * N_SEL
    tiles = pltpu.VMEM((PEER_GROUP, WORDS_PER_ROW * N_SEL, LANES), I32)
    return pl.pallas_call(
        _peer_v_kernel,
        grid=(t // tb,),
        in_specs=[
            pl.BlockSpec((tb, N_SEL), lambda i: (i, 0), memory_space=pltpu.SMEM),
            pl.BlockSpec((tb, wide), lambda i: (i, 0)),
            pl.BlockSpec((tb, wide), lambda i: (i, 0)),
            pl.BlockSpec(table.shape, lambda i: (0, 0)),
        ],
        out_specs=pl.BlockSpec((tb, wide), lambda i: (i, 0)),
        out_shape=jax.ShapeDtypeStruct(h2.shape, F32),
        scratch_shapes=[tiles, tiles],
        compiler_params=_cparams(("arbitrary",), VMEM_LIMIT_TABLE),
        name="peer_v",
    )(idx, w8, h2, table)


def _ple_kernel(h_ref, p_ref, gn_ref, wg_ref, bg_ref, wp_ref, fn_ref, o_ref, *, final):
    h = h_ref[...]
    n = (_rms(h, 1e-6) * gn_ref[...]).astype(BF16)
    z = _dot(n, wg_ref[...]) + bg_ref[...]
    gate = 1.0 / (1.0 + jnp.exp(-z))
    h = h + gate * _dot(p_ref[...].astype(BF16), wp_ref[...])
    if final:
        h = _rms(h, 1e-6) * fn_ref[...]
    o_ref[...] = h


def _ple(h2, p2, gn, wg, bg, wp, fn, final):
    t, d = h2.shape
    pd = p2.shape[1]
    row = lambda i: (i, 0)
    full = lambda i: (0, 0)
    return pl.pallas_call(
        functools.partial(_ple_kernel, final=final),
        grid=(t // ROW_TILE,),
        in_specs=[
            pl.BlockSpec((ROW_TILE, d), row),
            pl.BlockSpec((ROW_TILE, pd), row),
            pl.BlockSpec((1, d), full),
            pl.BlockSpec((d, d), full),
            pl.BlockSpec((1, d), full),
            pl.BlockSpec((pd, d), full),
            pl.BlockSpec((1, d), full),
        ],
        out_specs=pl.BlockSpec((ROW_TILE, d), row),
        out_shape=jax.ShapeDtypeStruct((t, d), F32),
        compiler_params=_cparams(("parallel",)),
        name="ple_final",
    )(h2, p2, gn, wg, bg, wp, fn)


def _permute_w_in(w):
    d = w.shape[0]
    hq, dq = DIFF_HEADS, DIFF_QK_DIM
    n_qk = 2 * hq * dq
    n_v = hq * DIFF_V_DIM
    swap_qk = lambda c: c.reshape(d, 2, hq, dq).transpose(0, 2, 1, 3).reshape(d, n_qk)
    wa = jnp.concatenate([swap_qk(w[:, :n_qk]), swap_qk(w[:, n_qk:2 * n_qk]), w[:, 2 * n_qk:2 * n_qk + n_v]],
                         axis=1)
    base = 2 * n_qk + n_v
    hr, dr = RET_HEADS, RET_QK_DIM
    n_r = hr * dr
    swap_r = lambda c: c.reshape(d, hr, 2, dr // 2).transpose(0, 2, 1, 3).reshape(d, n_r)
    wr = jnp.concatenate([swap_r(w[:, base:base + n_r]), swap_r(w[:, base + n_r:base + 2 * n_r]),
                          w[:, base + 2 * n_r:]], axis=1)
    return wa.astype(BF16), wr.astype(BF16)


def kernel(x, p, attn_norm, w_in, lam_q1, lam_k1, lam_q2, lam_k2, subln_gain, w_out, rel_bias, ffn_norm,
           peer_query, peer_subkeys, peer_u, peer_v, ple_norm, ple_gate_w, ple_gate_b, ple_proj, final_norm):
    b, s, d = x.shape
    t = b * s
    depth = w_in.shape[0]
    bias = _bias_tiles(rel_bias, ATT_TILE)
    n_diff = DIFF_HEADS * DIFF_V_DIM
    ex = _expand_matrix()
    h = x.reshape(t, d)
    for layer in range(depth):
        lambda_init = 0.8 - 0.6 * math.exp(-0.3 * layer)
        wa, wr = _permute_w_in(w_in[layer])
        pa, pr = _in_proj(h, attn_norm[layer][None, :], wa, wr)

        lamv = jnp.stack([lam_q1[layer], lam_k1[layer], lam_q2[layer], lam_k2[layer]]).astype(F32)
        diff_out = _diff_attn(pa.reshape(b, s, -1), bias, lamv, subln_gain[layer][None, :].astype(F32),
                              lambda_init)
        ret_out = _retention(pr.reshape(b, s, -1), s)

        wo = w_out[layer].astype(BF16)
        h, m, qry = _out_proj(h, diff_out.reshape(t, -1), ret_out.reshape(t, -1), wo[:n_diff], wo[n_diff:],
                              ffn_norm[layer][None, :], peer_query[layer].astype(BF16))

        subkeys = peer_subkeys[layer].reshape(2 * PEER_HEADS, PEER_N_KEYS, PEER_KEY_DIM).astype(BF16)
        idx_t, gate_t = _peer_topk(qry, subkeys)
        idx = idx_t.T
        gates = gate_t.T
        w8 = _peer_u(idx, m, gates, _pack_table(peer_u[layer]), ex, ex.T)
        h = _peer_v(idx, w8, h, _pack_table(peer_v[layer]))

        h = _ple(h, p[layer].reshape(t, -1), ple_norm[layer][None, :], ple_gate_w[layer].astype(BF16),
                 ple_gate_b[layer][None, :], ple_proj[layer].astype(BF16), final_norm[None, :],
                 final=(layer == depth - 1))
    return h.reshape(b, s, d)
```
